```python
import math
import jax, jax.numpy as jnp
from jax import lax
import numpy as np

D_MODEL = 1024
BATCH = 8
SEQ = 2048
DEPTH = 4
DEC_BATCH = 128
DEC_SEQ = 4
PAST_LEN = 8192
PAGE_SIZE = 128

W_A = D_MODEL // 2
POOL_WINDOWS = (2, 4, 8, 16)
POOL_GROUP = W_A // len(POOL_WINDOWS)
POOL_BUF = max(POOL_WINDOWS) - 1
HEAD_DIM = 64
N_HEADS = (D_MODEL // 2) // HEAD_DIM
KV_HEADS = 2
Q_PER_KV = N_HEADS // KV_HEADS
W_B = N_HEADS * HEAD_DIM
KV_W = KV_HEADS * HEAD_DIM
WINDOW = 128
BLOCK = WINDOW
ROT_DIM = HEAD_DIM // 4
ROPE_THETA = 500000.0
W_C = D_MODEL // 2
SSM_CH = 16
SSM_GROUPS = W_C // SSM_CH
SSM_STATE = 64
N_BRANCH = 3
D_FF = -(-(8 * D_MODEL // 3) // 256) * 256
EPS = 1e-6
IN_SPLITS = (W_A, W_A + W_B, W_A + W_B + KV_W, W_A + W_B + 2 * KV_W, W_A + W_B + 2 * KV_W + W_C)
IN_COLS = W_A + W_B + 2 * KV_W + W_C + N_BRANCH * D_MODEL

kernel_name = "gated_hybrid_pool_swa_s5_decoder_step"


def _rmsnorm(x, g):
    xf = x.astype(jnp.float32)
    r = lax.rsqrt(jnp.mean(xf * xf, axis=-1, keepdims=True) + EPS)
    return (xf * r).astype(x.dtype) * g


def _rope(x, pos):
    f32 = jnp.float32
    inv = ROPE_THETA ** (-jnp.arange(0, ROT_DIM, 2, dtype=f32) / ROT_DIM)
    ang = pos.astype(f32)[:, None] * inv[None, :]
    cos = jnp.cos(ang)[None, :, None, :]
    sin = jnp.sin(ang)[None, :, None, :]
    xr = x[..., :ROT_DIM].astype(f32)
    x1, x2 = xr[..., : ROT_DIM // 2], xr[..., ROT_DIM // 2:]
    rot = jnp.concatenate([x1 * cos - x2 * sin, x2 * cos + x1 * sin], axis=-1).astype(x.dtype)
    return jnp.concatenate([rot, x[..., ROT_DIM:]], axis=-1)


def _sink_attend(s, mask, sinks, v, eq_pv):
    s = jnp.where(mask, s, -jnp.inf)
    sk = sinks.astype(jnp.float32).reshape(KV_HEADS, Q_PER_KV, 1, 1)
    m = jnp.maximum(jnp.max(s, axis=-1, keepdims=True), sk)
    p = jnp.exp(s - m)
    den = jnp.sum(p, axis=-1, keepdims=True) + jnp.exp(sk - m)
    return jnp.einsum(eq_pv, (p / den).astype(v.dtype), v)


def _swa_prompt(q, k, v, sinks):
    bsz, seq = q.shape[:2]
    nb = seq // BLOCK
    qb = q.reshape(bsz, nb, BLOCK, KV_HEADS, Q_PER_KV, HEAD_DIM)

    def band(t):
        tb = t.reshape(bsz, nb, BLOCK, KV_HEADS, HEAD_DIM)
        prev = jnp.concatenate([jnp.zeros_like(tb[:, :1]), tb[:, :-1]], axis=1)
        return jnp.concatenate([prev, tb], axis=2)

    kb, vb = band(k), band(v)
    s = jnp.einsum("bnqgrd,bnkgd->bngrqk", qb, kb, preferred_element_type=jnp.float32) * (HEAD_DIM ** -0.5)
    qi = jnp.arange(BLOCK)[:, None]
    kj = jnp.arange(2 * BLOCK)[None, :]
    diff = BLOCK + qi - kj
    in_band = (diff >= 0) & (diff < WINDOW)
    blk = jnp.arange(nb)[:, None, None]
    mask = in_band[None] & ((blk > 0) | (kj >= BLOCK)[None])
    mask = mask[None, :, None, None]
    o = _sink_attend(s, mask, sinks, vb, "bngrqk,bnkgd->bnqgrd")
    return o.reshape(bsz, seq, W_B), k[:, -WINDOW:], v[:, -WINDOW:]


def _swa_sample(q, k, v, ck, cv, pos, sinks):
    bsz, t = q.shape[:2]
    wb = ck.shape[1]
    ke = jnp.concatenate([ck, k], axis=1)
    ve = jnp.concatenate([cv, v], axis=1)
    kpos = PAST_LEN - wb + jnp.arange(wb + t, dtype=jnp.int32)
    diff = pos[:, None] - kpos[None, :]
    mask = (diff >= 0) & (diff < WINDOW)
    qg = q.reshape(bsz, t, KV_HEADS, Q_PER_KV, HEAD_DIM)
    s = jnp.einsum("bqgrd,bkgd->bgrqk", qg, ke, preferred_element_type=jnp.float32) * (HEAD_DIM ** -0.5)
    o = _sink_attend(s, mask, sinks, ve, "bgrqk,bkgd->bqgrd")
    return o.reshape(bsz, t, W_B), ke[:, -wb:], ve[:, -wb:]


def _pool_mix(xa, prev, pos, pool_w, pool_scale):
    t = xa.shape[1]
    xe = jnp.concatenate([prev, xa], axis=1)
    cs = jnp.cumsum(xe.astype(jnp.float32), axis=1)
    cs = jnp.concatenate([jnp.zeros_like(cs[:, :1]), cs], axis=1)
    end = cs[:, POOL_BUF + 1:]
    outs = []
    for g, w in enumerate(POOL_WINDOWS):
        lo, hi = g * POOL_GROUP, (g + 1) * POOL_GROUP
        start = cs[:, POOL_BUF + 1 - w: POOL_BUF + 1 - w + t, lo:hi]
        cnt = jnp.minimum(pos + 1, w).astype(jnp.float32)[None, :, None]
        d = ((end[..., lo:hi] - start) / cnt - xa[..., lo:hi].astype(jnp.float32)).astype(xa.dtype)
        outs.append(d @ pool_w[g])
    y = jnp.concatenate(outs, axis=-1) * pool_scale
    return y, xe[:, -POOL_BUF:]


def _lin_combine(left, right):
    return (left[0] * right[0], right[0] * left[1] + right[1])


def _s5(u, h0_re, h0_im, a_re, a_im, log_dt, b_re, b_im, c_re, c_im, d, w_glu):
    f32 = jnp.float32
    bsz, t = u.shape[:2]
    uf = u.astype(f32).reshape(bsz, t, SSM_GROUPS, SSM_CH)
    a = lax.complex(a_re.astype(f32), a_im.astype(f32))
    dt = jnp.exp(log_dt.astype(f32))[:, None]
    a_bar = jnp.exp(a * dt)
    b = lax.complex(b_re.astype(f32), b_im.astype(f32))
    b_bar = ((a_bar - 1.0) / a)[..., None] * b
    drive = jnp.einsum("gpc,btgc->btgp", b_bar, uf.astype(jnp.complex64))
    h0 = lax.complex(h0_re.astype(f32), h0_im.astype(f32))
    drive = drive.at[:, 0].add(a_bar[None] * h0)
    decay = jnp.broadcast_to(a_bar, drive.shape)
    _, h = lax.associative_scan(_lin_combine, (decay, drive), axis=1)
    cm = lax.complex(c_re.astype(f32), c_im.astype(f32))
    y = jnp.einsum("gcp,btgp->btgc", cm, h).real + d.astype(f32).reshape(SSM_GROUPS, SSM_CH) * uf
    y = jax.nn.gelu(y.reshape(bsz, t, W_C)).astype(u.dtype)
    y = y * jax.nn.sigmoid(y @ w_glu)
    h_last = h[:, -1]
    return y, h_last.real.astype(h0_re.dtype), h_last.imag.astype(h0_re.dtype)


def _layer(x, c, pos, win_k, win_v, pool_prev, h0_re, h0_im, lp):
    bsz, t, _ = x.shape
    if pool_prev is None:
        pool_prev = jnp.zeros((bsz, POOL_BUF, W_A), x.dtype)
        h0_re = jnp.zeros((bsz, SSM_GROUPS, SSM_STATE), x.dtype)
        h0_im = jnp.zeros((bsz, SSM_GROUPS, SSM_STATE), x.dtype)
    mod = jax.nn.silu(c) @ lp["w_ada"] + lp["b_ada"]
    sh1, sc1, g1, sh2, sc2, g2 = [m[:, None, :] for m in jnp.split(mod, 6, axis=-1)]
    h = _rmsnorm(x, lp["norm1_g"]) * (1.0 + sc1) + sh1
    z = h @ lp["w_in"]
    xa, q, k, v, u, gates = jnp.split(z, list(IN_SPLITS), axis=-1)
    q = _rope(q.reshape(bsz, t, N_HEADS, HEAD_DIM), pos)
    k = _rope(k.reshape(bsz, t, KV_HEADS, HEAD_DIM), pos)
    v = v.reshape(bsz, t, KV_HEADS, HEAD_DIM)
    ya, pool_new = _pool_mix(xa, pool_prev, pos, lp["pool_w"], lp["pool_scale"])
    if win_k is None:
        yb, k_new, v_new = _swa_prompt(q, k, v, lp["attn_sinks"])
    else:
        yb, k_new, v_new = _swa_sample(q, k, v, win_k, win_v, pos, lp["attn_sinks"])
    yc, hre, him = _s5(u, h0_re, h0_im, lp["ssm_a_re"], lp["ssm_a_im"], lp["ssm_log_dt"], lp["ssm_b_re"],
                       lp["ssm_b_im"], lp["ssm_c_re"], lp["ssm_c_im"], lp["ssm_d"], lp["w_glu"])
    ga, gb, gc = jnp.split(jax.nn.sigmoid(gates), N_BRANCH, axis=-1)
    merged = ga * (ya @ lp["w_branch_a"]) + gb * (yb @ lp["w_branch_b"]) + gc * (yc @ lp["w_branch_c"])
    x = x + g1 * (merged @ lp["w_out"])
    h2 = _rmsnorm(x, lp["norm2_g"]) * (1.0 + sc2) + sh2
    a_up, b_up = jnp.split(h2 @ lp["w_ffn_in"], 2, axis=-1)
    x = x + g2 * ((jax.nn.silu(a_up) * b_up) @ lp["w_ffn_out"])
    return x, k_new, v_new, pool_new, hre, him


def setup_inputs(seed: int = 0) -> dict:
    key = jax.random.key(seed)
    ks = iter(jax.random.split(key, 48))
    f32 = jnp.float32

    def nrm(shape, s=1.0):
        return s * jax.random.normal(next(ks), shape, f32)

    L, D = DEPTH, D_MODEL
    win = min(WINDOW, PAST_LEN)
    n_idx = jnp.arange(SSM_STATE, dtype=f32)
    return {
        "x_prompt": nrm((BATCH, SEQ, D)),
        "x_sample": nrm((DEC_BATCH, DEC_SEQ, D)),
        "cache_win_k": nrm((L, DEC_BATCH, win, KV_HEADS, HEAD_DIM)),
        "cache_win_v": nrm((L, DEC_BATCH, win, KV_HEADS, HEAD_DIM)),
        "state_pool": nrm((L, DEC_BATCH, POOL_BUF, W_A)),
        "state_ssm_re": nrm((L, DEC_BATCH, SSM_GROUPS, SSM_STATE), 0.5),
        "state_ssm_im": nrm((L, DEC_BATCH, SSM_GROUPS, SSM_STATE), 0.5),
        "c_prompt": nrm((BATCH, D)),
        "c_sample": nrm((DEC_BATCH, D)),
        "norm1_g": 1.0 + nrm((L, D), 0.05),
        "norm2_g": 1.0 + nrm((L, D), 0.05),
        "w_ada": nrm((L, D, 6 * D), 0.5 * D ** -0.5),
        "b_ada": nrm((L, 6 * D), 0.02),
        "w_in": nrm((L, D, IN_COLS), D ** -0.5),
        "pool_w": nrm((L, len(POOL_WINDOWS), POOL_GROUP, POOL_GROUP), POOL_GROUP ** -0.5),
        "pool_scale": 1.0 + nrm((L, W_A), 0.1),
        "attn_sinks": nrm((L, N_HEADS), 0.5),
        "ssm_a_re": -0.5 * jnp.exp(nrm((L, SSM_GROUPS, SSM_STATE), 0.05)),
        "ssm_a_im": math.pi * n_idx + nrm((L, SSM_GROUPS, SSM_STATE), 0.05),
        "ssm_log_dt": jax.random.uniform(next(ks), (L, SSM_GROUPS), f32, math.log(1e-3), math.log(1e-1)),
        "ssm_b_re": nrm((L, SSM_GROUPS, SSM_STATE, SSM_CH), (2 * SSM_CH) ** -0.5),
        "ssm_b_im": nrm((L, SSM_GROUPS, SSM_STATE, SSM_CH), (2 * SSM_CH) ** -0.5),
        "ssm_c_re": nrm((L, SSM_GROUPS, SSM_CH, SSM_STATE), SSM_STATE ** -0.5),
        "ssm_c_im": nrm((L, SSM_GROUPS, SSM_CH, SSM_STATE), SSM_STATE ** -0.5),
        "ssm_d": nrm((L, W_C)),
        "w_glu": nrm((L, W_C, W_C), W_C ** -0.5),
        "w_branch_a": nrm((L, W_A, D), W_A ** -0.5),
        "w_branch_b": nrm((L, W_B, D), W_B ** -0.5),
        "w_branch_c": nrm((L, W_C, D), W_C ** -0.5),
        "w_out": nrm((L, D, D), D ** -0.5),
        "w_ffn_in": nrm((L, D, 2 * D_FF), D ** -0.5),
        "w_ffn_out": nrm((L, D_FF, D), D_FF ** -0.5),
        "final_norm_g": 1.0 + nrm((D,), 0.05),
    }


def reference(x_prompt, x_sample, cache_win_k, cache_win_v, state_pool, state_ssm_re, state_ssm_im,
              c_prompt, c_sample, norm1_g, norm2_g, w_ada, b_ada, w_in, pool_w, pool_scale, attn_sinks,
              ssm_a_re, ssm_a_im, ssm_log_dt, ssm_b_re, ssm_b_im, ssm_c_re, ssm_c_im, ssm_d, w_glu,
              w_branch_a, w_branch_b, w_branch_c, w_out, w_ffn_in, w_ffn_out, final_norm_g):
    pos_p = jnp.arange(x_prompt.shape[1], dtype=jnp.int32)
    pos_s = PAST_LEN + jnp.arange(x_sample.shape[1], dtype=jnp.int32)
    hp, hs = x_prompt, x_sample
    pk, pv, ppool, pre, pim = [], [], [], [], []
    sk, sv, spool, sre, sim = [], [], [], [], []
    for l in range(DEPTH):
        lp = dict(norm1_g=norm1_g[l], norm2_g=norm2_g[l], w_ada=w_ada[l], b_ada=b_ada[l], w_in=w_in[l],
                  pool_w=pool_w[l], pool_scale=pool_scale[l], attn_sinks=attn_sinks[l],
                  ssm_a_re=ssm_a_re[l], ssm_a_im=ssm_a_im[l], ssm_log_dt=ssm_log_dt[l],
                  ssm_b_re=ssm_b_re[l], ssm_b_im=ssm_b_im[l], ssm_c_re=ssm_c_re[l], ssm_c_im=ssm_c_im[l],
                  ssm_d=ssm_d[l], w_glu=w_glu[l], w_branch_a=w_branch_a[l], w_branch_b=w_branch_b[l],
                  w_branch_c=w_branch_c[l], w_out=w_out[l], w_ffn_in=w_ffn_in[l], w_ffn_out=w_ffn_out[l])
        hp, k1, v1, po1, r1, i1 = _layer(hp, c_prompt, pos_p, None, None, None, None, None, lp)
        pk.append(k1); pv.append(v1); ppool.append(po1); pre.append(r1); pim.append(i1)
        hs, k2, v2, po2, r2, i2 = _layer(hs, c_sample, pos_s, cache_win_k[l], cache_win_v[l], state_pool[l],
                                         state_ssm_re[l], state_ssm_im[l], lp)
        sk.append(k2); sv.append(v2); spool.append(po2); sre.append(r2); sim.append(i2)
    y_prompt = _rmsnorm(hp, final_norm_g)
    y_sample = _rmsnorm(hs, final_norm_g)
    return (y_prompt, y_sample,
            jnp.stack(pk), jnp.stack(pv), jnp.stack(ppool), jnp.stack(pre), jnp.stack(pim),
            jnp.stack(sk), jnp.stack(sv), jnp.stack(spool), jnp.stack(sre), jnp.stack(sim))
```

```python
import functools
import math

import jax
import jax.numpy as jnp
from jax import lax
from jax.experimental import pallas as pl
from jax.experimental.pallas import tpu as pltpu

F32 = jnp.float32
BF16 = jnp.bfloat16

D_MODEL = 1024
BATCH = 8
SEQ = 2048
DEPTH = 4
DEC_BATCH = 128
DEC_SEQ = 4
PAST_LEN = 8192

W_A = 512
POOL_WINDOWS = (2, 4, 8, 16)
POOL_GROUP = 128
POOL_BUF = 15
HEAD_DIM = 64
N_HEADS = 8
KV_HEADS = 2
W_B = 512
KV_W = 128
WINDOW = 128
ROT_DIM = 16
ROPE_THETA = 500000.0
W_C = 512
SSM_CH = 16
SSM_GROUPS = 32
SSM_STATE = 64
N_STATE = SSM_GROUPS * SSM_STATE
D_FF = 2816
EPS = 1e-6
MIX_COLS = W_A + W_B + 2 * KV_W + W_C

LANES = 128
SUBLANES = 8
VMEM_LIMIT = 56 * 1024 * 1024

TM = 512
NT = SEQ // TM
ATT_BLK = WINDOW
SSM_TC = 64
SCAN_CW = 512
NEG = -1e30
FF_CHUNKS = ((0, 1024), (1024, 1024), (2048, 768))


def _cparams(sem):
    return pltpu.CompilerParams(dimension_semantics=sem, vmem_limit_bytes=VMEM_LIMIT)


def _dot(a, b):
    return jnp.dot(a, b, preferred_element_type=F32)


def _norm_mod(x, g, sc, sh):
    r = lax.rsqrt(jnp.mean(x * x, axis=-1, keepdims=True) + EPS)
    return (x * r) * g * (1.0 + sc) + sh


def _rope(x, cos, sa, sb):
    return x * cos + pltpu.roll(x, 8, axis=1) * sa + pltpu.roll(x, LANES - 8, axis=1) * sb


def _ada_kernel(c_ref, w_ref, b_ref, o_ref):
    c = c_ref[...]
    a = (c * jax.nn.sigmoid(c)).astype(BF16)
    o_ref[...] = _dot(a, w_ref[...].astype(BF16)) + b_ref[...]


def _ada_call(c_all, w_ada, b_ada):
    n = c_all.shape[0]
    cb = 1536
    return pl.pallas_call(
        _ada_kernel,
        grid=(DEPTH, 6 * D_MODEL // cb),
        in_specs=[
            pl.BlockSpec((n, D_MODEL), lambda l, j: (0, 0)),
            pl.BlockSpec((None, D_MODEL, cb), lambda l, j: (l, 0, j)),
            pl.BlockSpec((None, 1, cb), lambda l, j: (l, 0, j)),
        ],
        out_specs=pl.BlockSpec((None, n, cb), lambda l, j: (l, 0, j)),
        out_shape=jax.ShapeDtypeStruct((DEPTH, n, 6 * D_MODEL), F32),
        compiler_params=_cparams(("arbitrary", "arbitrary")),
        name="ada",
    )(c_all, w_ada, b_ada.reshape(DEPTH, 1, 6 * D_MODEL))


def _proj_prompt_kernel(x_ref, sh_ref, sc_ref, g_ref, w_ref, pw_ref, ps_ref, cos_ref, sa_ref, sb_ref,
                        ya_ref, q_ref, kv_ref, u_ref, kwin_ref, vwin_ref, pool_ref, xe_ref):
    t = pl.program_id(1)
    tm = x_ref.shape[0]
    h = _norm_mod(x_ref[...], g_ref[...], sc_ref[...], sh_ref[...]).astype(BF16)

    xa = _dot(h, w_ref[:, 0:W_A])

    @pl.when(t == 0)
    def _():
        xe_ref[0:16, :] = jnp.zeros((16, W_A), F32)

    xe_ref[16:16 + tm, :] = xa
    pos = t * tm + lax.broadcasted_iota(jnp.int32, (tm, 1), 0)
    for g, w in enumerate(POOL_WINDOWS):
        lo = g * POOL_GROUP
        s = xe_ref[16:16 + tm, lo:lo + POOL_GROUP]
        for j in range(1, w):
            s = s + xe_ref[16 - j:16 - j + tm, lo:lo + POOL_GROUP]
        cnt = jnp.minimum(pos + 1, w).astype(F32)
        d = (s / cnt - xa[:, lo:lo + POOL_GROUP]).astype(BF16)
        y = _dot(d, pw_ref[g]) * ps_ref[:, lo:lo + POOL_GROUP]
        ya_ref[:, lo:lo + POOL_GROUP] = y.astype(BF16)

    @pl.when(t == pl.num_programs(1) - 1)
    def _():
        pool_ref[...] = xe_ref[tm + 1:tm + 16, :]

    xe_ref[0:16, :] = xe_ref[tm:tm + 16, :]

    cos = cos_ref[...]
    sa = sa_ref[...]
    sb = sb_ref[...]
    for c in range(W_B // LANES):
        qc = _dot(h, w_ref[:, W_A + c * LANES:W_A + (c + 1) * LANES])
        q_ref[:, c * LANES:(c + 1) * LANES] = (_rope(qc, cos, sa, sb) * (HEAD_DIM ** -0.5)).astype(BF16)
    kz = _rope(_dot(h, w_ref[:, W_A + W_B:W_A + W_B + KV_W]), cos, sa, sb)
    vz = _dot(h, w_ref[:, W_A + W_B + KV_W:W_A + W_B + 2 * KV_W])
    kv_ref[:, 0:128] = kz.astype(BF16)
    kv_ref[:, 128:256] = pltpu.roll(kz, HEAD_DIM, axis=1).astype(BF16)
    kv_ref[:, 256:384] = vz.astype(BF16)
    kv_ref[:, 384:512] = pltpu.roll(vz, HEAD_DIM, axis=1).astype(BF16)

    @pl.when(t == pl.num_programs(1) - 1)
    def _():
        kwin_ref[...] = kz[tm - WINDOW:, :]
        vwin_ref[...] = vz[tm - WINDOW:, :]

    u_ref[...] = _dot(h, w_ref[:, W_A + W_B + 2 * KV_W:MIX_COLS])


def _proj_prompt_call(x, mod, g1, w_mix, pool_w, pool_scale, rope_tabs):
    rows = BATCH * SEQ
    row_blk = lambda b, t: (b * NT + t, 0)
    const2 = lambda b, t: (0, 0)
    mod_spec = lambda j: pl.BlockSpec((None, 1, D_MODEL), lambda b, t: (b, 0, j))
    tab_spec = pl.BlockSpec((TM, LANES), lambda b, t: (t, 0))
    return pl.pallas_call(
        _proj_prompt_kernel,
        grid=(BATCH, NT),
        in_specs=[
            pl.BlockSpec((None, TM, D_MODEL), lambda b, t: (b, t, 0)),
            mod_spec(0), mod_spec(1),
            pl.BlockSpec((1, D_MODEL), const2),
            pl.BlockSpec((D_MODEL, MIX_COLS), const2),
            pl.BlockSpec((len(POOL_WINDOWS), POOL_GROUP, POOL_GROUP), lambda b, t: (0, 0, 0)),
            pl.BlockSpec((1, W_A), const2),
            tab_spec, tab_spec, tab_spec,
        ],
        out_specs=[
            pl.BlockSpec((TM, W_A), row_blk),
            pl.BlockSpec((TM, W_B), row_blk),
            pl.BlockSpec((TM, 4 * KV_W), row_blk),
            pl.BlockSpec((TM, W_C), lambda b, t: (t, b)),
            pl.BlockSpec((None, WINDOW, KV_W), lambda b, t: (b, 0, 0)),
            pl.BlockSpec((None, WINDOW, KV_W), lambda b, t: (b, 0, 0)),
            pl.BlockSpec((None, POOL_BUF, W_A), lambda b, t: (b, 0, 0)),
        ],
        out_shape=[
            jax.ShapeDtypeStruct((rows, W_A), BF16),
            jax.ShapeDtypeStruct((rows, W_B), BF16),
            jax.ShapeDtypeStruct((rows, 4 * KV_W), BF16),
            jax.ShapeDtypeStruct((SEQ, BATCH * W_C), F32),
            jax.ShapeDtypeStruct((BATCH, WINDOW, KV_W), F32),
            jax.ShapeDtypeStruct((BATCH, WINDOW, KV_W), F32),
            jax.ShapeDtypeStruct((BATCH, POOL_BUF, W_A), F32),
        ],
        scratch_shapes=[pltpu.VMEM((TM + 16, W_A), F32)],
        compiler_params=_cparams(("arbitrary", "arbitrary")),
        name="proj_prompt",
    )(x, mod, mod, g1, w_mix, pool_w, pool_scale, *rope_tabs)


def _head_pair_operand(lo_mask, a, b):
    zero = jnp.zeros_like(a)
    return jnp.concatenate([jnp.where(lo_mask, a, zero), jnp.where(lo_mask, zero, b)], axis=0)


def _sink_softmax(s, mask, sk):
    s = jnp.where(mask, s, NEG)
    m = jnp.maximum(jnp.max(s, axis=-1, keepdims=True), sk)
    p = jnp.exp(s - m)
    den = jnp.sum(p, axis=-1, keepdims=True) + jnp.exp(sk - m)
    return p / den


def _attn_prompt_kernel(q_ref, kvc_ref, kvp_ref, sink_ref, yb_ref):
    i = pl.program_id(1)
    nq = q_ref.shape[0]
    nk = 2 * nq
    kv = jnp.concatenate([kvp_ref[...], kvc_ref[...]], axis=0)
    lo_mask = lax.broadcasted_iota(jnp.int32, (nk, LANES), 1) < HEAD_DIM
    qi = lax.broadcasted_iota(jnp.int32, (nq, nk), 0)
    kj = lax.broadcasted_iota(jnp.int32, (nq, nk), 1)
    diff = nq + qi - kj
    mask = (diff >= 0) & (diff < WINDOW) & ((kj >= nq) | (i > 0))
    for g in range(KV_HEADS):
        ka, kb = (kv[:, 0:128], kv[:, 128:256]) if g == 0 else (kv[:, 128:256], kv[:, 0:128])
        va, vb = (kv[:, 256:384], kv[:, 384:512]) if g == 0 else (kv[:, 384:512], kv[:, 256:384])
        kk = _head_pair_operand(lo_mask, ka, kb)
        vv = _head_pair_operand(lo_mask, va, vb)
        for jj in range(2):
            j = 2 * g + jj
            s = lax.dot_general(q_ref[:, j * LANES:(j + 1) * LANES], kk, (((1,), (1,)), ((), ())),
                                preferred_element_type=F32)
            ps = []
            for hh in range(2):
                sk = sink_ref[j:j + 1, hh * LANES:hh * LANES + 1]
                ps.append(_sink_softmax(s[:, hh * nk:(hh + 1) * nk], mask, sk).astype(BF16))
            o = _dot(jnp.concatenate(ps, axis=1), vv)
            yb_ref[:, j * LANES:(j + 1) * LANES] = o.astype(BF16)


def _attn_prompt_call(q, kv, sink_rows):
    nb = SEQ // ATT_BLK
    return pl.pallas_call(
        _attn_prompt_kernel,
        grid=(BATCH, nb),
        in_specs=[
            pl.BlockSpec((ATT_BLK, W_B), lambda b, i: (b * nb + i, 0)),
            pl.BlockSpec((ATT_BLK, 4 * KV_W), lambda b, i: (b * nb + i, 0)),
            pl.BlockSpec((ATT_BLK, 4 * KV_W), lambda b, i: (b * nb + jnp.maximum(i - 1, 0), 0)),
            pl.BlockSpec((N_HEADS // 2, 2 * LANES), lambda b, i: (0, 0)),
        ],
        out_specs=pl.BlockSpec((ATT_BLK, W_B), lambda b, i: (b * nb + i, 0)),
        out_shape=jax.ShapeDtypeStruct((BATCH * SEQ, W_B), BF16),
        compiler_params=_cparams(("arbitrary", "arbitrary")),
        name="attn_prompt",
    )(q, kv, kv, sink_rows)


def _ssm_kernel(rows_per_step, u_ref, h0re_ref, h0im_ref, are_ref, aim_ref, bre_ref, bim_ref, cre_ref, cimn_ref,
                d_ref, wglu_ref, yc_ref, hre_out, him_out, dre, dim_, hre, him):
    i = pl.program_id(0)
    r = rows_per_step
    tc = u_ref.shape[0] // r
    half = N_STATE // 2

    @pl.when(i == 0)
    def _():
        hre[...] = h0re_ref[...]
        him[...] = h0im_ref[...]

    u = u_ref[...]
    ub = u.astype(BF16)
    for kt in range(2):
        uk = ub[:, kt * 256:(kt + 1) * 256]
        dre[:, kt * half:(kt + 1) * half] = _dot(uk, bre_ref[kt])
        dim_[:, kt * half:(kt + 1) * half] = _dot(uk, bim_ref[kt])

    for rt in range(r // SUBLANES):
        for c in range(N_STATE // SCAN_CW):
            cols = slice(c * SCAN_CW, (c + 1) * SCAN_CW)
            rsl = slice(rt * SUBLANES, (rt + 1) * SUBLANES)
            ar = jnp.broadcast_to(are_ref[:, cols], (SUBLANES, SCAN_CW))
            ai = jnp.broadcast_to(aim_ref[:, cols], (SUBLANES, SCAN_CW))

            def step(t, carry, cols=cols, rt=rt, ar=ar, ai=ai):
                pr, pi = carry
                row = pl.multiple_of(t * r + rt * SUBLANES, SUBLANES)
                nr = ar * pr - ai * pi + dre[pl.ds(row, SUBLANES), cols]
                ni = ar * pi + ai * pr + dim_[pl.ds(row, SUBLANES), cols]
                dre[pl.ds(row, SUBLANES), cols] = nr
                dim_[pl.ds(row, SUBLANES), cols] = ni
                return nr, ni

            fr, fi = lax.fori_loop(0, tc, step, (hre[rsl, cols], him[rsl, cols]), unroll=min(tc, 4))
            hre[rsl, cols] = fr
            him[rsl, cols] = fi

    @pl.when(i == pl.num_programs(0) - 1)
    def _():
        hre_out[...] = hre[...]
        him_out[...] = him[...]

    ys = []
    for nt in range(2):
        sl = slice(nt * half, (nt + 1) * half)
        ys.append(_dot(dre[:, sl].astype(BF16), cre_ref[nt]) + _dot(dim_[:, sl].astype(BF16), cimn_ref[nt]))
    y = jnp.concatenate(ys, axis=1) + d_ref[...] * u
    y = jax.nn.gelu(y)
    yc_ref[...] = (y * jax.nn.sigmoid(_dot(y.astype(BF16), wglu_ref[...]))).astype(BF16)


def _ssm_call(u, h0re, h0im, sp, d, w_glu, rows_per_step, steps_per_block, name):
    nrows = u.shape[0]
    blk = rows_per_step * steps_per_block
    const2 = lambda i: (0, 0)
    const3 = lambda i: (0, 0, 0)
    half = N_STATE // 2
    return pl.pallas_call(
        functools.partial(_ssm_kernel, rows_per_step),
        grid=(nrows // blk,),
        in_specs=[
            pl.BlockSpec((blk, W_C), lambda i: (i, 0)),
            pl.BlockSpec((rows_per_step, N_STATE), const2),
            pl.BlockSpec((rows_per_step, N_STATE), const2),
            pl.BlockSpec((1, N_STATE), const2),
            pl.BlockSpec((1, N_STATE), const2),
            pl.BlockSpec((2, 256, half), const3),
            pl.BlockSpec((2, 256, half), const3),
            pl.BlockSpec((2, half, 256), const3),
            pl.BlockSpec((2, half, 256), const3),
            pl.BlockSpec((1, W_C), const2),
            pl.BlockSpec((W_C, W_C), const2),
        ],
        out_specs=[
            pl.BlockSpec((blk, W_C), lambda i: (i, 0)),
            pl.BlockSpec((rows_per_step, N_STATE), const2),
            pl.BlockSpec((rows_per_step, N_STATE), const2),
        ],
        out_shape=[
            jax.ShapeDtypeStruct((nrows, W_C), BF16),
            jax.ShapeDtypeStruct((rows_per_step, N_STATE), F32),
            jax.ShapeDtypeStruct((rows_per_step, N_STATE), F32),
        ],
        scratch_shapes=[
            pltpu.VMEM((blk, N_STATE), F32),
            pltpu.VMEM((blk, N_STATE), F32),
            pltpu.VMEM((rows_per_step, N_STATE), F32),
            pltpu.VMEM((rows_per_step, N_STATE), F32),
        ],
        compiler_params=_cparams(("arbitrary",)),
        name=name,
    )(u, h0re, h0im, sp["a_re"], sp["a_im"], sp["b_re"], sp["b_im"], sp["c_re"], sp["c_imn"], d, w_glu)


def _merge_kernel(sample, x_ref, sh_ref, sc_ref, gt_ref, g_ref, ya_ref, yb_ref, yc_ref,
                  wg_ref, wa_ref, wb_ref, wc_ref, wo_ref, o_ref):
    x = x_ref[...]
    h = _norm_mod(x, g_ref[...], sc_ref[...], sh_ref[...]).astype(BF16)
    if sample:
        t = pl.program_id(1)
        parts = []
        for j in range(N_HEADS // 2):
            start = (j // 2) * 8 + t * 2 + (j % 2)
            parts.append(yb_ref[pl.ds(start, DEC_BATCH, stride=16), :])
        yb = jnp.concatenate(parts, axis=1).astype(BF16)
    else:
        yb = yb_ref[...]
    merged = None
    for k, (y, w_ref) in enumerate(((ya_ref[...], wa_ref), (yb, wb_ref), (yc_ref[...], wc_ref))):
        gate = jax.nn.sigmoid(_dot(h, wg_ref[:, k * D_MODEL:(k + 1) * D_MODEL]))
        term = gate * _dot(y, w_ref[...])
        merged = term if merged is None else merged + term
    o_ref[...] = x + gt_ref[...] * _dot(merged.astype(BF16), wo_ref[...])


def _row_specs(sample):
    if sample:
        tm = DEC_BATCH
        grid = (1, DEC_SEQ)
        x_spec = pl.BlockSpec((tm, D_MODEL), lambda b, t: (0, t))
        mod_spec = lambda j: pl.BlockSpec((tm, D_MODEL), lambda b, t: (0, j))
        row_spec = lambda cols: pl.BlockSpec((tm, cols), lambda b, t: (t, 0))
        yc_spec = row_spec(W_C)
    else:
        tm = TM
        grid = (BATCH, NT)
        x_spec = pl.BlockSpec((None, tm, D_MODEL), lambda b, t: (b, t, 0))
        mod_spec = lambda j: pl.BlockSpec((None, 1, D_MODEL), lambda b, t: (b, 0, j))
        row_spec = lambda cols: pl.BlockSpec((tm, cols), lambda b, t: (b * NT + t, 0))
        yc_spec = pl.BlockSpec((tm, W_C), lambda b, t: (t, b))
    return tm, grid, x_spec, mod_spec, row_spec, yc_spec


def _full_spec(shape):
    nd = len(shape)
    return pl.BlockSpec(shape, lambda b, t: (0,) * nd)


def _merge_call(sample, x, mod, g1, ya, yb, yc, wts):
    tm, grid, x_spec, mod_spec, row_spec, yc_spec = _row_specs(sample)
    yb_spec = _full_spec(yb.shape) if sample else row_spec(W_B)
    return pl.pallas_call(
        functools.partial(_merge_kernel, sample),
        grid=grid,
        in_specs=[
            x_spec, mod_spec(0), mod_spec(1), mod_spec(2),
            _full_spec((1, D_MODEL)),
            row_spec(W_A), yb_spec, yc_spec,
            _full_spec((D_MODEL, 3 * D_MODEL)),
            _full_spec((W_A, D_MODEL)), _full_spec((W_B, D_MODEL)), _full_spec((W_C, D_MODEL)),
            _full_spec((D_MODEL, D_MODEL)),
        ],
        out_specs=x_spec,
        out_shape=jax.ShapeDtypeStruct(x.shape, F32),
        compiler_params=_cparams(("arbitrary", "arbitrary")),
        name="merge_sample" if sample else "merge_prompt",
    )(x, mod, mod, mod, g1, ya, yb, yc, wts["w_gate"], wts["w_a"], wts["w_b"], wts["w_c"], wts["w_out"])


def _ffn_kernel(final, x_ref, sh_ref, sc_ref, gt_ref, g_ref, wi_ref, wo_ref, gf_ref, o_ref):
    x = x_ref[...]
    h = _norm_mod(x, g_ref[...], sc_ref[...], sh_ref[...]).astype(BF16)
    acc = None
    for lo, n in FF_CHUNKS:
        a = _dot(h, wi_ref[:, lo:lo + n])
        b = _dot(h, wi_ref[:, D_FF + lo:D_FF + lo + n])
        act = ((a * jax.nn.sigmoid(a)) * b).astype(BF16)
        part = _dot(act, wo_ref[lo:lo + n, :])
        acc = part if acc is None else acc + part
    y = x + gt_ref[...] * acc
    if final:
        r = lax.rsqrt(jnp.mean(y * y, axis=-1, keepdims=True) + EPS)
        y = (y * r) * gf_ref[...]
    o_ref[...] = y


def _ffn_call(sample, final, x, mod, g2, w_ffn_in, w_ffn_out, g_final):
    tm, grid, x_spec, mod_spec, _, _ = _row_specs(sample)
    single = pl.Buffered(1)
    return pl.pallas_call(
        functools.partial(_ffn_kernel, final),
        grid=grid,
        in_specs=[
            x_spec, mod_spec(3), mod_spec(4), mod_spec(5),
            _full_spec((1, D_MODEL)),
            pl.BlockSpec((D_MODEL, 2 * D_FF), lambda b, t: (0, 0), pipeline_mode=single),
            pl.BlockSpec((D_FF, D_MODEL), lambda b, t: (0, 0), pipeline_mode=single),
            _full_spec((1, D_MODEL)),
        ],
        out_specs=x_spec,
        out_shape=jax.ShapeDtypeStruct(x.shape, F32),
        compiler_params=_cparams(("arbitrary", "arbitrary")),
        name="ffn_sample" if sample else "ffn_prompt",
    )(x, mod, mod, mod, g2, w_ffn_in, w_ffn_out, g_final)


def _proj_sample_kernel(x_ref, mod_ref, g_ref, w_ref, pw_ref, ps_ref, cos_ref, sa_ref, sb_ref, pool_prev_ref,
                        ya_ref, q_ref, kn_ref, vn_ref, u_ref, pool_ref):
    nb = DEC_BATCH
    sh = mod_ref[:, 0:D_MODEL]
    sc = mod_ref[:, D_MODEL:2 * D_MODEL]
    h = jnp.concatenate(
        [_norm_mod(x_ref[:, t * D_MODEL:(t + 1) * D_MODEL], g_ref[...], sc, sh) for t in range(DEC_SEQ)],
        axis=0).astype(BF16)

    xa = _dot(h, w_ref[:, 0:W_A])
    xe = [pool_prev_ref[:, j * W_A:(j + 1) * W_A] for j in range(POOL_BUF)]
    xe += [xa[t * nb:(t + 1) * nb, :] for t in range(DEC_SEQ)]
    for j in range(POOL_BUF):
        pool_ref[:, j * W_A:(j + 1) * W_A] = xe[DEC_SEQ + j]
    for g, w in enumerate(POOL_WINDOWS):
        lo = g * POOL_GROUP
        ds = []
        for t in range(DEC_SEQ):
            s = xe[POOL_BUF + t][:, lo:lo + POOL_GROUP]
            for j in range(1, w):
                s = s + xe[POOL_BUF + t - j][:, lo:lo + POOL_GROUP]
            ds.append(s / float(w) - xe[POOL_BUF + t][:, lo:lo + POOL_GROUP])
        d = jnp.concatenate(ds, axis=0).astype(BF16)
        y = _dot(d, pw_ref[g]) * ps_ref[:, lo:lo + POOL_GROUP]
        ya_ref[:, lo:lo + POOL_GROUP] = y.astype(BF16)

    def tabs(t):
        return cos_ref[t:t + 1, :], sa_ref[t:t + 1, :], sb_ref[t:t + 1, :]

    for j in range(W_B // LANES):
        qc = _dot(h, w_ref[:, W_A + j * LANES:W_A + (j + 1) * LANES])
        for t in range(DEC_SEQ):
            qt = _rope(qc[t * nb:(t + 1) * nb, :], *tabs(t)) * (HEAD_DIM ** -0.5)
            q_ref[pl.ds((j // 2) * 8 + t * 2 + (j % 2), nb, stride=16), :] = qt
    kz = _dot(h, w_ref[:, W_A + W_B:W_A + W_B + KV_W])
    vz = _dot(h, w_ref[:, W_A + W_B + KV_W:W_A + W_B + 2 * KV_W])
    for t in range(DEC_SEQ):
        kn_ref[pl.ds(t, nb, stride=DEC_SEQ), :] = _rope(kz[t * nb:(t + 1) * nb, :], *tabs(t))
        vn_ref[pl.ds(t, nb, stride=DEC_SEQ), :] = vz[t * nb:(t + 1) * nb, :]
    u_ref[...] = _dot(h, w_ref[:, W_A + W_B + 2 * KV_W:MIX_COLS])


def _proj_sample_call(x2d, mod, g1, w_mix, pool_w, pool_scale, rope_tabs, pool_prev):
    rows = DEC_BATCH * DEC_SEQ
    full = lambda shape: pl.BlockSpec(shape, lambda i: (0,) * len(shape))
    args = (x2d, mod, g1, w_mix, pool_w, pool_scale, *rope_tabs, pool_prev)
    out_shape = [
        jax.ShapeDtypeStruct((rows, W_A), BF16),
        jax.ShapeDtypeStruct((DEC_BATCH * 16, LANES), F32),
        jax.ShapeDtypeStruct((rows, KV_W), F32),
        jax.ShapeDtypeStruct((rows, KV_W), F32),
        jax.ShapeDtypeStruct((rows, W_C), F32),
        jax.ShapeDtypeStruct((DEC_BATCH, POOL_BUF * W_A), F32),
    ]
    return pl.pallas_call(
        _proj_sample_kernel,
        grid=(1,),
        in_specs=[full(a.shape) for a in args],
        out_specs=[full(s.shape) for s in out_shape],
        out_shape=out_shape,
        compiler_params=_cparams(("arbitrary",)),
        name="proj_sample",
    )(*args)


ATT_SEQS = 8


def _attn_sample_kernel(q_ref, kc_ref, vc_ref, kn_ref, vn_ref, sink_ref, yb_ref, ko_ref, vo_ref):
    nn = 2 * SUBLANES
    lo_c = lax.broadcasted_iota(jnp.int32, (WINDOW, LANES), 1) < HEAD_DIM
    lo_n = lax.broadcasted_iota(jnp.int32, (SUBLANES, LANES), 1) < HEAD_DIM
    tq_c = jnp.right_shift(lax.broadcasted_iota(jnp.int32, (SUBLANES, WINDOW), 0), 1)
    mask_c = lax.broadcasted_iota(jnp.int32, (SUBLANES, WINDOW), 1) > tq_c
    tq_n = jnp.right_shift(lax.broadcasted_iota(jnp.int32, (SUBLANES, nn), 0), 1)
    col_n = lax.broadcasted_iota(jnp.int32, (SUBLANES, nn), 1)
    vis_n = jnp.bitwise_and(col_n, SUBLANES - 1) <= tq_n
    half_n = jnp.right_shift(col_n, 3)
    pad = jnp.zeros((SUBLANES - DEC_SEQ, LANES), F32)
    nt_dims = (((1,), (1,)), ((), ()))
    for i in range(ATT_SEQS):
        kc = kc_ref[i]
        vc = vc_ref[i]
        kn = kn_ref[i * DEC_SEQ:(i + 1) * DEC_SEQ, :]
        vn = vn_ref[i * DEC_SEQ:(i + 1) * DEC_SEQ, :]
        ko_ref[i, 0:WINDOW - DEC_SEQ, :] = kc_ref[i, DEC_SEQ:WINDOW, :]
        ko_ref[i, WINDOW - DEC_SEQ:WINDOW, :] = kn
        vo_ref[i, 0:WINDOW - DEC_SEQ, :] = vc_ref[i, DEC_SEQ:WINDOW, :]
        vo_ref[i, WINDOW - DEC_SEQ:WINDOW, :] = vn
        kcr = pltpu.roll(kc, HEAD_DIM, axis=1)
        vcr = pltpu.roll(vc, HEAD_DIM, axis=1)
        kn8 = jnp.concatenate([kn, pad], axis=0)
        vn8 = jnp.concatenate([vn, pad], axis=0)
        knr = pltpu.roll(kn8, HEAD_DIM, axis=1)
        vnr = pltpu.roll(vn8, HEAD_DIM, axis=1)
        for g in range(KV_HEADS):
            sel = (lambda a, b: (a, b)) if g == 0 else (lambda a, b: (b, a))
            kk = jnp.concatenate([_head_pair_operand(lo_c, *sel(kc, kcr)),
                                  _head_pair_operand(lo_n, *sel(kn8, knr))], axis=0).astype(BF16)
            vv = jnp.concatenate([_head_pair_operand(lo_c, *sel(vc, vcr)),
                                  _head_pair_operand(lo_n, *sel(vn8, vnr))], axis=0).astype(BF16)
            row0 = i * 16 + g * 8
            qg = q_ref[row0:row0 + 8, :].astype(BF16)
            s = lax.dot_general(qg, kk, nt_dims, preferred_element_type=F32)
            s_n = s[:, 2 * WINDOW:2 * WINDOW + nn]
            pcs = []
            pn = jnp.zeros((SUBLANES, nn), F32)
            for hh in range(2):
                sk = sink_ref[g * 8:(g + 1) * 8, hh * LANES:hh * LANES + 1]
                sc = jnp.where(mask_c, s[:, hh * WINDOW:(hh + 1) * WINDOW], NEG)
                sn = jnp.where(vis_n & (half_n == hh), s_n, NEG)
                m = jnp.maximum(jnp.maximum(jnp.max(sc, axis=-1, keepdims=True),
                                            jnp.max(sn, axis=-1, keepdims=True)), sk)
                pc = jnp.exp(sc - m)
                ph = jnp.exp(sn - m)
                den = (jnp.sum(pc, axis=-1, keepdims=True) + jnp.sum(ph, axis=-1, keepdims=True)
                       + jnp.exp(sk - m))
                pcs.append(pc / den)
                pn = pn + ph / den
            p = jnp.concatenate(pcs + [pn], axis=1).astype(BF16)
            yb_ref[row0:row0 + 8, :] = _dot(p, vv)


def _attn_sample_call(q, kc, vc, kn, vn, sink_rows):
    bs = ATT_SEQS
    cache_spec = pl.BlockSpec((bs, WINDOW, KV_W), lambda i: (i, 0, 0))
    new_spec = pl.BlockSpec((bs * DEC_SEQ, KV_W), lambda i: (i, 0))
    q_spec = pl.BlockSpec((bs * 16, LANES), lambda i: (i, 0))
    return pl.pallas_call(
        _attn_sample_kernel,
        grid=(DEC_BATCH // bs,),
        in_specs=[q_spec, cache_spec, cache_spec, new_spec, new_spec,
                  pl.BlockSpec((16, 2 * LANES), lambda i: (0, 0))],
        out_specs=[q_spec, cache_spec, cache_spec],
        out_shape=[
            jax.ShapeDtypeStruct((DEC_BATCH * 16, LANES), F32),
            jax.ShapeDtypeStruct((DEC_BATCH, WINDOW, KV_W), F32),
            jax.ShapeDtypeStruct((DEC_BATCH, WINDOW, KV_W), F32),
        ],
        compiler_params=_cparams(("arbitrary",)),
        name="attn_sample",
    )(q, kc, vc, kn, vn, sink_rows)


def _rope_tables(pos):
    inv = ROPE_THETA ** (-jnp.arange(0, ROT_DIM, 2, dtype=F32) / ROT_DIM)
    ang = pos.astype(F32)[:, None] * inv[None, :]
    cos, sin = jnp.cos(ang), jnp.sin(ang)
    n = pos.shape[0]
    half = ROT_DIM // 2
    rest = HEAD_DIM - ROT_DIM
    z8 = jnp.zeros((n, half), F32)
    zr = jnp.zeros((n, rest), F32)
    c64 = jnp.concatenate([cos, cos, jnp.ones((n, rest), F32)], axis=1)
    sa64 = jnp.concatenate([z8, sin, zr], axis=1)
    sb64 = jnp.concatenate([-sin, z8, zr], axis=1)
    return tuple(jnp.tile(t, (1, LANES // HEAD_DIM)) for t in (c64, sa64, sb64))


def _ssm_params(a_re, a_im, log_dt, b_re, b_im, c_re, c_im):
    a = lax.complex(a_re.astype(F32), a_im.astype(F32))
    dt = jnp.exp(log_dt.astype(F32))[:, None]
    a_bar = jnp.exp(a * dt)
    b = lax.complex(b_re.astype(F32), b_im.astype(F32))
    b_bar = ((a_bar - 1.0) / a)[..., None] * b
    eye = jnp.eye(SSM_GROUPS, dtype=F32)
    half = N_STATE // 2

    def diag_in(m):
        t = jnp.einsum("gpc,gh->gchp", m, eye).reshape(W_C, N_STATE)
        return jnp.stack([t[0:256, 0:half], t[256:512, half:N_STATE]]).astype(BF16)

    def diag_out(m):
        t = jnp.einsum("gcp,gh->gphc", m, eye).reshape(N_STATE, W_C)
        return jnp.stack([t[0:half, 0:256], t[half:N_STATE, 256:512]]).astype(BF16)

    return dict(
        a_re=jnp.real(a_bar).reshape(1, N_STATE), a_im=jnp.imag(a_bar).reshape(1, N_STATE),
        b_re=diag_in(jnp.real(b_bar)), b_im=diag_in(jnp.imag(b_bar)),
        c_re=diag_out(c_re.astype(F32)), c_imn=diag_out(-c_im.astype(F32)),
    )


def _sink_rows_prompt(sinks):
    return jnp.repeat(sinks.astype(F32).reshape(N_HEADS // 2, 2), LANES, axis=1)


def _sink_rows_sample(sinks):
    pairs = _sink_rows_prompt(sinks).reshape(KV_HEADS, 1, 2, 2 * LANES)
    return jnp.broadcast_to(pairs, (KV_HEADS, DEC_SEQ, 2, 2 * LANES)).reshape(16, 2 * LANES)


def kernel(x_prompt, x_sample, cache_win_k, cache_win_v, state_pool, state_ssm_re, state_ssm_im, c_prompt, c_sample, norm1_g, norm2_g, w_ada, b_ada, w_in, pool_w, pool_scale, attn_sinks, ssm_a_re, ssm_a_im, ssm_log_dt, ssm_b_re, ssm_b_im, ssm_c_re, ssm_c_im, ssm_d, w_glu, w_branch_a, w_branch_b, w_branch_c, w_out, w_ffn_in, w_ffn_out, final_norm_g):
    mod_all = _ada_call(jnp.concatenate([c_prompt, c_sample], axis=0), w_ada, b_ada)
    tabs_p = _rope_tables(jnp.arange(SEQ, dtype=jnp.int32))
    tabs_s = _rope_tables(PAST_LEN + jnp.arange(DEC_SEQ, dtype=jnp.int32))
    g_final = final_norm_g.reshape(1, D_MODEL)
    zeros_state = jnp.zeros((BATCH, N_STATE), F32)

    hp = x_prompt
    hs = x_sample.reshape(DEC_BATCH, DEC_SEQ * D_MODEL)
    outs = [[] for _ in range(10)]
    for l in range(DEPTH):
        mod_p = mod_all[l, :BATCH].reshape(BATCH, 1, 6 * D_MODEL)
        mod_s = mod_all[l, BATCH:]
        g1 = norm1_g[l].reshape(1, D_MODEL)
        g2 = norm2_g[l].reshape(1, D_MODEL)
        w_mix = w_in[l, :, :MIX_COLS].astype(BF16)
        wts = dict(w_gate=w_in[l, :, MIX_COLS:].astype(BF16), w_a=w_branch_a[l].astype(BF16),
                   w_b=w_branch_b[l].astype(BF16), w_c=w_branch_c[l].astype(BF16), w_out=w_out[l].astype(BF16))
        pw = pool_w[l].astype(BF16)
        ps = pool_scale[l].reshape(1, W_A)
        sp = _ssm_params(ssm_a_re[l], ssm_a_im[l], ssm_log_dt[l], ssm_b_re[l], ssm_b_im[l], ssm_c_re[l], ssm_c_im[l])
        d_skip = ssm_d[l].reshape(1, W_C)
        wglu = w_glu[l].astype(BF16)
        wfi = w_ffn_in[l].astype(BF16)
        wfo = w_ffn_out[l].astype(BF16)
        last = l == DEPTH - 1

        ya, q, kv, u, kwin, vwin, pool_new = _proj_prompt_call(hp, mod_p, g1, w_mix, pw, ps, tabs_p)
        yb = _attn_prompt_call(q, kv, _sink_rows_prompt(attn_sinks[l]))
        yc, hre, him = _ssm_call(u.reshape(SEQ * BATCH, W_C), zeros_state, zeros_state, sp, d_skip, wglu,
                                 BATCH, SSM_TC, "ssm_prompt")
        hp = _merge_call(False, hp, mod_p, g1, ya, yb, yc.reshape(SEQ, BATCH * W_C), wts)
        hp = _ffn_call(False, last, hp, mod_p, g2, wfi, wfo, g_final)
        for k, v in zip(range(5), (kwin.reshape(BATCH, WINDOW, KV_HEADS, HEAD_DIM),
                                   vwin.reshape(BATCH, WINDOW, KV_HEADS, HEAD_DIM), pool_new,
                                   hre.reshape(BATCH, SSM_GROUPS, SSM_STATE),
                                   him.reshape(BATCH, SSM_GROUPS, SSM_STATE))):
            outs[k].append(v)

        ya, q, kn, vn, u, pool_new = _proj_sample_call(
            hs, mod_s, g1, w_mix, pw, ps, tabs_s, state_pool[l].reshape(DEC_BATCH, POOL_BUF * W_A))
        yb, k_new, v_new = _attn_sample_call(
            q, cache_win_k[l].reshape(DEC_BATCH, WINDOW, KV_W), cache_win_v[l].reshape(DEC_BATCH, WINDOW, KV_W),
            kn, vn, _sink_rows_sample(attn_sinks[l]))
        yc, hre, him = _ssm_call(u, state_ssm_re[l].reshape(DEC_BATCH, N_STATE),
                                 state_ssm_im[l].reshape(DEC_BATCH, N_STATE), sp, d_skip, wglu,
                                 DEC_BATCH, DEC_SEQ, "ssm_sample")
        hs = _merge_call(True, hs, mod_s, g1, ya, yb, yc, wts)
        hs = _ffn_call(True, last, hs, mod_s, g2, wfi, wfo, g_final)
        for k, v in zip(range(5, 10), (k_new.reshape(DEC_BATCH, WINDOW, KV_HEADS, HEAD_DIM),
                                       v_new.reshape(DEC_BATCH, WINDOW, KV_HEADS, HEAD_DIM),
                                       pool_new.reshape(DEC_BATCH, POOL_BUF, W_A),
                                       hre.reshape(DEC_BATCH, SSM_GROUPS, SSM_STATE),
                                       him.reshape(DEC_BATCH, SSM_GROUPS, SSM_STATE))):
            outs[k].append(v)

    return (hp, hs.reshape(DEC_BATCH, DEC_SEQ, D_MODEL), *[jnp.stack(o) for o in outs])
```

```python
import functools
import math

import jax
import jax.numpy as jnp
from jax import lax
from jax.experimental import pallas as pl
from jax.experimental.pallas import tpu as pltpu

F32 = jnp.float32
BF16 = jnp.bfloat16

D_MODEL = 1024
BATCH = 8
SEQ = 2048
DEPTH = 4
DEC_BATCH = 128
DEC_SEQ = 4
PAST_LEN = 8192

W_A = 512
POOL_WINDOWS = (2, 4, 8, 16)
POOL_GROUP = 128
POOL_BUF = 15
HEAD_DIM = 64
N_HEADS = 8
KV_HEADS = 2
W_B = 512
KV_W = 128
WINDOW = 128
ROT_DIM = 16
ROPE_THETA = 500000.0
W_C = 512
SSM_CH = 16
SSM_GROUPS = 32
SSM_STATE = 64
N_STATE = SSM_GROUPS * SSM_STATE
D_FF = 2816
EPS = 1e-6
MIX_COLS = W_A + W_B + 2 * KV_W + W_C

LANES = 128
SUBLANES = 8
VMEM_LIMIT = 56 * 1024 * 1024

TM = 512
NT = SEQ // TM
ATT_BLK = WINDOW
ATT_SUB = 2
SSM_TC = 64
SCAN_CW = 512
NEG = -1e30
FF_CHUNKS = ((0, 1024), (1024, 1024), (2048, 768))


def _cparams(sem):
    return pltpu.CompilerParams(dimension_semantics=sem, vmem_limit_bytes=VMEM_LIMIT)


def _dot(a, b):
    return jnp.dot(a, b, preferred_element_type=F32)


def _norm_mod(x, g, sc, sh):
    r = lax.rsqrt(jnp.mean(x * x, axis=-1, keepdims=True) + EPS)
    return (x * r) * g * (1.0 + sc) + sh


def _rope(x, cos, sa, sb):
    return x * cos + pltpu.roll(x, 8, axis=1) * sa + pltpu.roll(x, LANES - 8, axis=1) * sb


def _ada_kernel(c_ref, w_ref, b_ref, o_ref):
    c = c_ref[...]
    a = (c * jax.nn.sigmoid(c)).astype(BF16)
    o_ref[...] = _dot(a, w_ref[...].astype(BF16)) + b_ref[...]


def _ada_call(c_all, w_ada, b_ada):
    n = c_all.shape[0]
    cb = 1536
    return pl.pallas_call(
        _ada_kernel,
        grid=(DEPTH, 6 * D_MODEL // cb),
        in_specs=[
            pl.BlockSpec((n, D_MODEL), lambda l, j: (0, 0)),
            pl.BlockSpec((None, D_MODEL, cb), lambda l, j: (l, 0, j)),
            pl.BlockSpec((None, 1, cb), lambda l, j: (l, 0, j)),
        ],
        out_specs=pl.BlockSpec((None, n, cb), lambda l, j: (l, 0, j)),
        out_shape=jax.ShapeDtypeStruct((DEPTH, n, 6 * D_MODEL), F32),
        compiler_params=_cparams(("arbitrary", "arbitrary")),
        name="ada",
    )(c_all, w_ada, b_ada.reshape(DEPTH, 1, 6 * D_MODEL))


def _proj_prompt_kernel(x_ref, sh_ref, sc_ref, g_ref, w_ref, pw_ref, ps_ref, cos_ref, sa_ref, sb_ref,
                        ya_ref, q_ref, kv_ref, u_ref, kwin_ref, vwin_ref, pool_ref, xe_ref):
    t = pl.program_id(1)
    tm = x_ref.shape[0]
    h = _norm_mod(x_ref[...], g_ref[...], sc_ref[...], sh_ref[...]).astype(BF16)

    xa = _dot(h, w_ref[:, 0:W_A])

    @pl.when(t == 0)
    def _():
        xe_ref[0:16, :] = jnp.zeros((16, W_A), F32)

    xe_ref[16:16 + tm, :] = xa
    pos = t * tm + lax.broadcasted_iota(jnp.int32, (tm, 1), 0)
    for g, w in enumerate(POOL_WINDOWS):
        lo = g * POOL_GROUP
        s = xe_ref[16:16 + tm, lo:lo + POOL_GROUP]
        for j in range(1, w):
            s = s + xe_ref[16 - j:16 - j + tm, lo:lo + POOL_GROUP]
        cnt = jnp.minimum(pos + 1, w).astype(F32)
        d = (s / cnt - xa[:, lo:lo + POOL_GROUP]).astype(BF16)
        y = _dot(d, pw_ref[g]) * ps_ref[:, lo:lo + POOL_GROUP]
        ya_ref[:, lo:lo + POOL_GROUP] = y.astype(BF16)

    @pl.when(t == pl.num_programs(1) - 1)
    def _():
        pool_ref[...] = xe_ref[tm + 1:tm + 16, :]

    xe_ref[0:16, :] = xe_ref[tm:tm + 16, :]

    cos = cos_ref[...]
    sa = sa_ref[...]
    sb = sb_ref[...]
    for c in range(W_B // LANES):
        qc = _dot(h, w_ref[:, W_A + c * LANES:W_A + (c + 1) * LANES])
        q_ref[:, c * LANES:(c + 1) * LANES] = (_rope(qc, cos, sa, sb) * (HEAD_DIM ** -0.5)).astype(BF16)
    kz = _rope(_dot(h, w_ref[:, W_A + W_B:W_A + W_B + KV_W]), cos, sa, sb)
    vz = _dot(h, w_ref[:, W_A + W_B + KV_W:W_A + W_B + 2 * KV_W])
    kv_ref[:, 0:128] = kz.astype(BF16)
    kv_ref[:, 128:256] = pltpu.roll(kz, HEAD_DIM, axis=1).astype(BF16)
    kv_ref[:, 256:384] = vz.astype(BF16)
    kv_ref[:, 384:512] = pltpu.roll(vz, HEAD_DIM, axis=1).astype(BF16)

    @pl.when(t == pl.num_programs(1) - 1)
    def _():
        kwin_ref[...] = kz[tm - WINDOW:, :]
        vwin_ref[...] = vz[tm - WINDOW:, :]

    u_ref[...] = _dot(h, w_ref[:, W_A + W_B + 2 * KV_W:MIX_COLS])


def _proj_prompt_call(x, mod, g1, w_mix, pool_w, pool_scale, rope_tabs):
    rows = BATCH * SEQ
    row_blk = lambda b, t: (b * NT + t, 0)
    const2 = lambda b, t: (0, 0)
    mod_spec = lambda j: pl.BlockSpec((None, 1, D_MODEL), lambda b, t: (b, 0, j))
    tab_spec = pl.BlockSpec((TM, LANES), lambda b, t: (t, 0))
    return pl.pallas_call(
        _proj_prompt_kernel,
        grid=(BATCH, NT),
        in_specs=[
            pl.BlockSpec((None, TM, D_MODEL), lambda b, t: (b, t, 0)),
            mod_spec(0), mod_spec(1),
            pl.BlockSpec((1, D_MODEL), const2),
            pl.BlockSpec((D_MODEL, MIX_COLS), const2),
            pl.BlockSpec((len(POOL_WINDOWS), POOL_GROUP, POOL_GROUP), lambda b, t: (0, 0, 0)),
            pl.BlockSpec((1, W_A), const2),
            tab_spec, tab_spec, tab_spec,
        ],
        out_specs=[
            pl.BlockSpec((TM, W_A), row_blk),
            pl.BlockSpec((TM, W_B), row_blk),
            pl.BlockSpec((TM, 4 * KV_W), row_blk),
            pl.BlockSpec((TM, W_C), row_blk),
            pl.BlockSpec((None, WINDOW, KV_W), lambda b, t: (b, 0, 0)),
            pl.BlockSpec((None, WINDOW, KV_W), lambda b, t: (b, 0, 0)),
            pl.BlockSpec((None, POOL_BUF, W_A), lambda b, t: (b, 0, 0)),
        ],
        out_shape=[
            jax.ShapeDtypeStruct((rows, W_A), BF16),
            jax.ShapeDtypeStruct((rows, W_B), BF16),
            jax.ShapeDtypeStruct((rows, 4 * KV_W), BF16),
            jax.ShapeDtypeStruct((rows, W_C), F32),
            jax.ShapeDtypeStruct((BATCH, WINDOW, KV_W), F32),
            jax.ShapeDtypeStruct((BATCH, WINDOW, KV_W), F32),
            jax.ShapeDtypeStruct((BATCH, POOL_BUF, W_A), F32),
        ],
        scratch_shapes=[pltpu.VMEM((TM + 16, W_A), F32)],
        compiler_params=_cparams(("arbitrary", "arbitrary")),
        name="proj_prompt",
    )(x, mod, mod, g1, w_mix, pool_w, pool_scale, *rope_tabs)


def _head_pair_operand(lo_mask, a, b):
    zero = jnp.zeros_like(a)
    return jnp.concatenate([jnp.where(lo_mask, a, zero), jnp.where(lo_mask, zero, b)], axis=0)


def _attn_window_block(q_ref, yb_ref, sink_ref, r0, kv_prev, kv_cur, prev_bias):
    n = ATT_BLK
    lo = lax.broadcasted_iota(jnp.int32, (n, LANES), 1) < HEAD_DIM
    rows = jnp.bitwise_and(lax.broadcasted_iota(jnp.int32, (2 * n, LANES), 0), n - 1)
    from_prev = lax.broadcasted_iota(jnp.int32, (2 * n, LANES), 1) > rows
    lo2 = lax.broadcasted_iota(jnp.int32, (2 * n, LANES), 1) < HEAD_DIM
    top = lax.broadcasted_iota(jnp.int32, (2 * n, 1), 0) < n
    zero = jnp.zeros((n, LANES), BF16)
    ones_a = jnp.where(lo, 1.0, 0.0).astype(BF16)
    ones_b = jnp.where(lo, 0.0, 1.0).astype(BF16)
    nt_dims = (((1,), (1,)), ((), ()))
    for g in range(KV_HEADS):
        ca, cb = (0, LANES) if g == 0 else (LANES, 0)
        kk = jnp.concatenate([jnp.where(lo, kv_prev[:, ca:ca + LANES], zero),
                              jnp.where(lo, kv_cur[:, ca:ca + LANES], zero),
                              jnp.where(lo, zero, kv_prev[:, cb:cb + LANES]),
                              jnp.where(lo, zero, kv_cur[:, cb:cb + LANES])], axis=0)
        va, vb = 2 * LANES + ca, 2 * LANES + cb
        vv = jnp.concatenate([
            jnp.concatenate([jnp.where(lo, kv_prev[:, va:va + LANES], zero), ones_a], axis=1),
            jnp.concatenate([jnp.where(lo, kv_cur[:, va:va + LANES], zero), ones_a], axis=1),
            jnp.concatenate([jnp.where(lo, zero, kv_prev[:, vb:vb + LANES]), ones_b], axis=1),
            jnp.concatenate([jnp.where(lo, zero, kv_cur[:, vb:vb + LANES]), ones_b], axis=1)], axis=0)
        qs = jnp.concatenate([q_ref[pl.ds(r0, n), (2 * g) * LANES:(2 * g + 1) * LANES],
                              q_ref[pl.ds(r0, n), (2 * g + 1) * LANES:(2 * g + 2) * LANES]], axis=0)
        s = lax.dot_general(qs, kk, nt_dims, preferred_element_type=F32)
        ps, es = [], []
        for hh in range(2):
            s_prev = s[:, hh * 2 * n:hh * 2 * n + n]
            if prev_bias is not None:
                s_prev = s_prev + prev_bias
            logits = jnp.where(from_prev, s_prev, s[:, hh * 2 * n + n:(hh + 1) * 2 * n])
            sk = jnp.where(top, sink_ref[4 * g + hh], sink_ref[4 * g + 2 + hh])
            m = jnp.maximum(jnp.max(logits, axis=-1, keepdims=True), sk)
            p = jnp.exp(logits - m)
            ps += [jnp.where(from_prev, p, 0.0), jnp.where(from_prev, 0.0, p)]
            es.append(jnp.exp(sk - m))
        r = _dot(jnp.concatenate(ps, axis=1).astype(BF16), vv)
        o = r[:, 0:LANES] / (r[:, LANES:2 * LANES] + jnp.where(lo2, es[0], es[1]))
        yb_ref[pl.ds(r0, n), (2 * g) * LANES:(2 * g + 1) * LANES] = o[0:n].astype(BF16)
        yb_ref[pl.ds(r0, n), (2 * g + 1) * LANES:(2 * g + 2) * LANES] = o[n:2 * n].astype(BF16)


def _attn_prompt_kernel(sink_ref, q_ref, kvc_ref, kvp_ref, yb_ref):
    i = pl.program_id(1)
    first_bias = jnp.where(i > 0, 0.0, NEG).astype(F32)
    for sub in range(ATT_SUB):
        r0 = sub * ATT_BLK
        kv_cur = kvc_ref[r0:r0 + ATT_BLK, :]
        kv_prev = kvp_ref[...] if sub == 0 else kvc_ref[r0 - ATT_BLK:r0, :]
        _attn_window_block(q_ref, yb_ref, sink_ref, r0, kv_prev, kv_cur, first_bias if sub == 0 else None)


def _attn_prompt_call(q, kv, sinks):
    rows = ATT_SUB * ATT_BLK
    nb = SEQ // rows
    return pl.pallas_call(
        _attn_prompt_kernel,
        grid=(BATCH, nb),
        in_specs=[
            pl.BlockSpec(memory_space=pltpu.SMEM),
            pl.BlockSpec((rows, W_B), lambda b, i: (b * nb + i, 0)),
            pl.BlockSpec((rows, 4 * KV_W), lambda b, i: (b * nb + i, 0)),
            pl.BlockSpec((ATT_BLK, 4 * KV_W), lambda b, i: ((b * nb + i) * ATT_SUB - jnp.minimum(i, 1), 0)),
        ],
        out_specs=pl.BlockSpec((rows, W_B), lambda b, i: (b * nb + i, 0)),
        out_shape=jax.ShapeDtypeStruct((BATCH * SEQ, W_B), BF16),
        compiler_params=_cparams(("arbitrary", "arbitrary")),
        name="attn_prompt",
    )(sinks, q, kv, kv)


def _ssm_kernel(rows_per_step, seq_major, u_ref, h0re_ref, h0im_ref, are_ref, aim_ref, bre_ref, bim_ref, cre_ref,
                cimn_ref, d_ref, wglu_ref, yc_ref, hre_out, him_out, dre, dim_, hre, him, *slabs):
    i = pl.program_id(0)
    r = rows_per_step
    half = N_STATE // 2
    nslab = W_C // LANES

    @pl.when(i == 0)
    def _():
        hre[...] = h0re_ref[...]
        him[...] = h0im_ref[...]

    if seq_major:
        (perm,) = slabs
        tc = u_ref.shape[1]
        for b in range(r):
            for c in range(nslab):
                perm[c, pl.ds(b, tc, stride=r), :] = u_ref[b, :, c * LANES:(c + 1) * LANES]
        ub = jnp.concatenate([perm[c] for c in range(nslab)], axis=1).astype(BF16)
    else:
        tc = u_ref.shape[0] // r
        ub = u_ref[...].astype(BF16)
    for kt in range(2):
        uk = ub[:, kt * 256:(kt + 1) * 256]
        dre[:, kt * half:(kt + 1) * half] = _dot(uk, bre_ref[kt])
        dim_[:, kt * half:(kt + 1) * half] = _dot(uk, bim_ref[kt])

    for rt in range(r // SUBLANES):
        for c in range(N_STATE // SCAN_CW):
            cols = slice(c * SCAN_CW, (c + 1) * SCAN_CW)
            rsl = slice(rt * SUBLANES, (rt + 1) * SUBLANES)
            ar = jnp.broadcast_to(are_ref[:, cols], (SUBLANES, SCAN_CW))
            ai = jnp.broadcast_to(aim_ref[:, cols], (SUBLANES, SCAN_CW))

            def step(t, carry, cols=cols, rt=rt, ar=ar, ai=ai):
                pr, pi = carry
                row = pl.multiple_of(t * r + rt * SUBLANES, SUBLANES)
                nr = ar * pr - ai * pi + dre[pl.ds(row, SUBLANES), cols]
                ni = ar * pi + ai * pr + dim_[pl.ds(row, SUBLANES), cols]
                dre[pl.ds(row, SUBLANES), cols] = nr
                dim_[pl.ds(row, SUBLANES), cols] = ni
                return nr, ni

            fr, fi = lax.fori_loop(0, tc, step, (hre[rsl, cols], him[rsl, cols]), unroll=min(tc, 4))
            hre[rsl, cols] = fr
            him[rsl, cols] = fi

    @pl.when(i == pl.num_programs(0) - 1)
    def _():
        hre_out[...] = hre[...]
        him_out[...] = him[...]

    ys = []
    for nt in range(2):
        sl = slice(nt * half, (nt + 1) * half)
        ys.append(_dot(dre[:, sl].astype(BF16), cre_ref[nt]) + _dot(dim_[:, sl].astype(BF16), cimn_ref[nt]))
    if seq_major:
        for c in range(nslab):
            perm[c] = ys[c // 2][:, (c % 2) * LANES:(c % 2 + 1) * LANES]
        y = jnp.concatenate(
            [jnp.concatenate([perm[c, pl.ds(b, tc, stride=r), :] for c in range(nslab)], axis=1) for b in range(r)],
            axis=0)
        u = u_ref[...].reshape(r * tc, W_C)
    else:
        y = jnp.concatenate(ys, axis=1)
        u = u_ref[...]
    y = jax.nn.gelu(y + d_ref[...] * u)
    yc = (y * jax.nn.sigmoid(_dot(y.astype(BF16), wglu_ref[...]))).astype(BF16)
    yc_ref[...] = yc.reshape(yc_ref.shape)


def _ssm_call(u, h0re, h0im, sp, d, w_glu, rows_per_step, steps_per_block, seq_major, name):
    blk = rows_per_step * steps_per_block
    const2 = lambda i: (0, 0)
    const3 = lambda i: (0, 0, 0)
    half = N_STATE // 2
    if seq_major:
        nsteps = u.shape[1] // steps_per_block
        io_spec = pl.BlockSpec((rows_per_step, steps_per_block, W_C), lambda i: (0, i, 0))
        scratch = [pltpu.VMEM((W_C // LANES, blk, LANES), F32)]
    else:
        nsteps = u.shape[0] // blk
        io_spec = pl.BlockSpec((blk, W_C), lambda i: (i, 0))
        scratch = []
    return pl.pallas_call(
        functools.partial(_ssm_kernel, rows_per_step, seq_major),
        grid=(nsteps,),
        in_specs=[
            io_spec,
            pl.BlockSpec((rows_per_step, N_STATE), const2),
            pl.BlockSpec((rows_per_step, N_STATE), const2),
            pl.BlockSpec((1, N_STATE), const2),
            pl.BlockSpec((1, N_STATE), const2),
            pl.BlockSpec((2, 256, half), const3),
            pl.BlockSpec((2, 256, half), const3),
            pl.BlockSpec((2, half, 256), const3),
            pl.BlockSpec((2, half, 256), const3),
            pl.BlockSpec((1, W_C), const2),
            pl.BlockSpec((W_C, W_C), const2),
        ],
        out_specs=[
            io_spec,
            pl.BlockSpec((rows_per_step, N_STATE), const2),
            pl.BlockSpec((rows_per_step, N_STATE), const2),
        ],
        out_shape=[
            jax.ShapeDtypeStruct(u.shape, BF16),
            jax.ShapeDtypeStruct((rows_per_step, N_STATE), F32),
            jax.ShapeDtypeStruct((rows_per_step, N_STATE), F32),
        ],
        scratch_shapes=[
            pltpu.VMEM((blk, N_STATE), F32),
            pltpu.VMEM((blk, N_STATE), F32),
            pltpu.VMEM((rows_per_step, N_STATE), F32),
            pltpu.VMEM((rows_per_step, N_STATE), F32),
        ] + scratch,
        compiler_params=_cparams(("arbitrary",)),
        name=name,
    )(u, h0re, h0im, sp["a_re"], sp["a_im"], sp["b_re"], sp["b_im"], sp["c_re"], sp["c_imn"], d, w_glu)


def _merge_kernel(sample, x_ref, sh_ref, sc_ref, gt_ref, g_ref, ya_ref, yb_ref, yc_ref,
                  wg_ref, wa_ref, wb_ref, wc_ref, wo_ref, o_ref):
    x = x_ref[...]
    h = _norm_mod(x, g_ref[...], sc_ref[...], sh_ref[...]).astype(BF16)
    if sample:
        t = pl.program_id(1)
        parts = []
        for j in range(N_HEADS // 2):
            start = (j // 2) * 8 + t * 2 + (j % 2)
            parts.append(yb_ref[pl.ds(start, DEC_BATCH, stride=16), :])
        yb = jnp.concatenate(parts, axis=1).astype(BF16)
    else:
        yb = yb_ref[...]
    merged = None
    for k, (y, w_ref) in enumerate(((ya_ref[...], wa_ref), (yb, wb_ref), (yc_ref[...], wc_ref))):
        gate = jax.nn.sigmoid(_dot(h, wg_ref[:, k * D_MODEL:(k + 1) * D_MODEL]))
        term = gate * _dot(y, w_ref[...])
        merged = term if merged is None else merged + term
    o_ref[...] = x + gt_ref[...] * _dot(merged.astype(BF16), wo_ref[...])


def _row_specs(sample):
    if sample:
        tm = DEC_BATCH
        grid = (1, DEC_SEQ)
        x_spec = pl.BlockSpec((tm, D_MODEL), lambda b, t: (0, t))
        mod_spec = lambda j: pl.BlockSpec((tm, D_MODEL), lambda b, t: (0, j))
        row_spec = lambda cols: pl.BlockSpec((tm, cols), lambda b, t: (t, 0))
        yc_spec = row_spec(W_C)
    else:
        tm = TM
        grid = (BATCH, NT)
        x_spec = pl.BlockSpec((None, tm, D_MODEL), lambda b, t: (b, t, 0))
        mod_spec = lambda j: pl.BlockSpec((None, 1, D_MODEL), lambda b, t: (b, 0, j))
        row_spec = lambda cols: pl.BlockSpec((tm, cols), lambda b, t: (b * NT + t, 0))
        yc_spec = row_spec(W_C)
    return tm, grid, x_spec, mod_spec, row_spec, yc_spec


def _full_spec(shape):
    nd = len(shape)
    return pl.BlockSpec(shape, lambda b, t: (0,) * nd)


def _merge_call(sample, x, mod, g1, ya, yb, yc, wts):
    tm, grid, x_spec, mod_spec, row_spec, yc_spec = _row_specs(sample)
    yb_spec = _full_spec(yb.shape) if sample else row_spec(W_B)
    return pl.pallas_call(
        functools.partial(_merge_kernel, sample),
        grid=grid,
        in_specs=[
            x_spec, mod_spec(0), mod_spec(1), mod_spec(2),
            _full_spec((1, D_MODEL)),
            row_spec(W_A), yb_spec, yc_spec,
            _full_spec((D_MODEL, 3 * D_MODEL)),
            _full_spec((W_A, D_MODEL)), _full_spec((W_B, D_MODEL)), _full_spec((W_C, D_MODEL)),
            _full_spec((D_MODEL, D_MODEL)),
        ],
        out_specs=x_spec,
        out_shape=jax.ShapeDtypeStruct(x.shape, F32),
        compiler_params=_cparams(("arbitrary", "arbitrary")),
        name="merge_sample" if sample else "merge_prompt",
    )(x, mod, mod, mod, g1, ya, yb, yc, wts["w_gate"], wts["w_a"], wts["w_b"], wts["w_c"], wts["w_out"])


def _ffn_kernel(final, x_ref, sh_ref, sc_ref, gt_ref, g_ref, wi_ref, wo_ref, gf_ref, o_ref):
    x = x_ref[...]
    h = _norm_mod(x, g_ref[...], sc_ref[...], sh_ref[...]).astype(BF16)
    acc = None
    for lo, n in FF_CHUNKS:
        a = _dot(h, wi_ref[:, lo:lo + n])
        b = _dot(h, wi_ref[:, D_FF + lo:D_FF + lo + n])
        act = ((a * jax.nn.sigmoid(a)) * b).astype(BF16)
        part = _dot(act, wo_ref[lo:lo + n, :])
        acc = part if acc is None else acc + part
    y = x + gt_ref[...] * acc
    if final:
        r = lax.rsqrt(jnp.mean(y * y, axis=-1, keepdims=True) + EPS)
        y = (y * r) * gf_ref[...]
    o_ref[...] = y


def _ffn_call(sample, final, x, mod, g2, w_ffn_in, w_ffn_out, g_final):
    tm, grid, x_spec, mod_spec, _, _ = _row_specs(sample)
    single = pl.Buffered(1)
    return pl.pallas_call(
        functools.partial(_ffn_kernel, final),
        grid=grid,
        in_specs=[
            x_spec, mod_spec(3), mod_spec(4), mod_spec(5),
            _full_spec((1, D_MODEL)),
            pl.BlockSpec((D_MODEL, 2 * D_FF), lambda b, t: (0, 0), pipeline_mode=single),
            pl.BlockSpec((D_FF, D_MODEL), lambda b, t: (0, 0), pipeline_mode=single),
            _full_spec((1, D_MODEL)),
        ],
        out_specs=x_spec,
        out_shape=jax.ShapeDtypeStruct(x.shape, F32),
        compiler_params=_cparams(("arbitrary", "arbitrary")),
        name="ffn_sample" if sample else "ffn_prompt",
    )(x, mod, mod, mod, g2, w_ffn_in, w_ffn_out, g_final)


def _proj_sample_kernel(x_ref, mod_ref, g_ref, w_ref, pw_ref, ps_ref, cos_ref, sa_ref, sb_ref, pool_prev_ref,
                        ya_ref, q_ref, kn_ref, vn_ref, u_ref, pool_ref):
    nb = DEC_BATCH
    sh = mod_ref[:, 0:D_MODEL]
    sc = mod_ref[:, D_MODEL:2 * D_MODEL]
    h = jnp.concatenate(
        [_norm_mod(x_ref[:, t * D_MODEL:(t + 1) * D_MODEL], g_ref[...], sc, sh) for t in range(DEC_SEQ)],
        axis=0).astype(BF16)

    xa = _dot(h, w_ref[:, 0:W_A])
    xe = [pool_prev_ref[:, j * W_A:(j + 1) * W_A] for j in range(POOL_BUF)]
    xe += [xa[t * nb:(t + 1) * nb, :] for t in range(DEC_SEQ)]
    for j in range(POOL_BUF):
        pool_ref[:, j * W_A:(j + 1) * W_A] = xe[DEC_SEQ + j]
    for g, w in enumerate(POOL_WINDOWS):
        lo = g * POOL_GROUP
        ds = []
        for t in range(DEC_SEQ):
            s = xe[POOL_BUF + t][:, lo:lo + POOL_GROUP]
            for j in range(1, w):
                s = s + xe[POOL_BUF + t - j][:, lo:lo + POOL_GROUP]
            ds.append(s / float(w) - xe[POOL_BUF + t][:, lo:lo + POOL_GROUP])
        d = jnp.concatenate(ds, axis=0).astype(BF16)
        y = _dot(d, pw_ref[g]) * ps_ref[:, lo:lo + POOL_GROUP]
        ya_ref[:, lo:lo + POOL_GROUP] = y.astype(BF16)

    def tabs(t):
        return cos_ref[t:t + 1, :], sa_ref[t:t + 1, :], sb_ref[t:t + 1, :]

    for j in range(W_B // LANES):
        qc = _dot(h, w_ref[:, W_A + j * LANES:W_A + (j + 1) * LANES])
        for t in range(DEC_SEQ):
            qt = _rope(qc[t * nb:(t + 1) * nb, :], *tabs(t)) * (HEAD_DIM ** -0.5)
            q_ref[pl.ds((j // 2) * 8 + t * 2 + (j % 2), nb, stride=16), :] = qt
    kz = _dot(h, w_ref[:, W_A + W_B:W_A + W_B + KV_W])
    vz = _dot(h, w_ref[:, W_A + W_B + KV_W:W_A + W_B + 2 * KV_W])
    for t in range(DEC_SEQ):
        kn_ref[pl.ds(t, nb, stride=DEC_SEQ), :] = _rope(kz[t * nb:(t + 1) * nb, :], *tabs(t))
        vn_ref[pl.ds(t, nb, stride=DEC_SEQ), :] = vz[t * nb:(t + 1) * nb, :]
    u_ref[...] = _dot(h, w_ref[:, W_A + W_B + 2 * KV_W:MIX_COLS])


def _proj_sample_call(x2d, mod, g1, w_mix, pool_w, pool_scale, rope_tabs, pool_prev):
    rows = DEC_BATCH * DEC_SEQ
    full = lambda shape: pl.BlockSpec(shape, lambda i: (0,) * len(shape))
    args = (x2d, mod, g1, w_mix, pool_w, pool_scale, *rope_tabs, pool_prev)
    out_shape = [
        jax.ShapeDtypeStruct((rows, W_A), BF16),
        jax.ShapeDtypeStruct((DEC_BATCH * 16, LANES), F32),
        jax.ShapeDtypeStruct((rows, KV_W), F32),
        jax.ShapeDtypeStruct((rows, KV_W), F32),
        jax.ShapeDtypeStruct((rows, W_C), F32),
        jax.ShapeDtypeStruct((DEC_BATCH, POOL_BUF * W_A), F32),
    ]
    return pl.pallas_call(
        _proj_sample_kernel,
        grid=(1,),
        in_specs=[full(a.shape) for a in args],
        out_specs=[full(s.shape) for s in out_shape],
        out_shape=out_shape,
        compiler_params=_cparams(("arbitrary",)),
        name="proj_sample",
    )(*args)


ATT_SEQS = 8


def _attn_sample_kernel(q_ref, kc_ref, vc_ref, kn_ref, vn_ref, sink_ref, yb_ref, ko_ref, vo_ref):
    nn = 2 * SUBLANES
    lo_c = lax.broadcasted_iota(jnp.int32, (WINDOW, LANES), 1) < HEAD_DIM
    lo_n = lax.broadcasted_iota(jnp.int32, (SUBLANES, LANES), 1) < HEAD_DIM
    tq_c = jnp.right_shift(lax.broadcasted_iota(jnp.int32, (SUBLANES, WINDOW), 0), 1)
    mask_c = lax.broadcasted_iota(jnp.int32, (SUBLANES, WINDOW), 1) > tq_c
    tq_n = jnp.right_shift(lax.broadcasted_iota(jnp.int32, (SUBLANES, nn), 0), 1)
    col_n = lax.broadcasted_iota(jnp.int32, (SUBLANES, nn), 1)
    vis_n = jnp.bitwise_and(col_n, SUBLANES - 1) <= tq_n
    half_n = jnp.right_shift(col_n, 3)
    pad = jnp.zeros((SUBLANES - DEC_SEQ, LANES), F32)
    nt_dims = (((1,), (1,)), ((), ()))
    for i in range(ATT_SEQS):
        kc = kc_ref[i]
        vc = vc_ref[i]
        kn = kn_ref[i * DEC_SEQ:(i + 1) * DEC_SEQ, :]
        vn = vn_ref[i * DEC_SEQ:(i + 1) * DEC_SEQ, :]
        ko_ref[i, 0:WINDOW - DEC_SEQ, :] = kc_ref[i, DEC_SEQ:WINDOW, :]
        ko_ref[i, WINDOW - DEC_SEQ:WINDOW, :] = kn
        vo_ref[i, 0:WINDOW - DEC_SEQ, :] = vc_ref[i, DEC_SEQ:WINDOW, :]
        vo_ref[i, WINDOW - DEC_SEQ:WINDOW, :] = vn
        kcr = pltpu.roll(kc, HEAD_DIM, axis=1)
        vcr = pltpu.roll(vc, HEAD_DIM, axis=1)
        kn8 = jnp.concatenate([kn, pad], axis=0)
        vn8 = jnp.concatenate([vn, pad], axis=0)
        knr = pltpu.roll(kn8, HEAD_DIM, axis=1)
        vnr = pltpu.roll(vn8, HEAD_DIM, axis=1)
        for g in range(KV_HEADS):
            sel = (lambda a, b: (a, b)) if g == 0 else (lambda a, b: (b, a))
            kk = jnp.concatenate([_head_pair_operand(lo_c, *sel(kc, kcr)),
                                  _head_pair_operand(lo_n, *sel(kn8, knr))], axis=0).astype(BF16)
            vv = jnp.concatenate([_head_pair_operand(lo_c, *sel(vc, vcr)),
                                  _head_pair_operand(lo_n, *sel(vn8, vnr))], axis=0).astype(BF16)
            row0 = i * 16 + g * 8
            qg = q_ref[row0:row0 + 8, :].astype(BF16)
            s = lax.dot_general(qg, kk, nt_dims, preferred_element_type=F32)
            s_n = s[:, 2 * WINDOW:2 * WINDOW + nn]
            pcs = []
            pn = jnp.zeros((SUBLANES, nn), F32)
            for hh in range(2):
                sk = sink_ref[g * 8:(g + 1) * 8, hh * LANES:hh * LANES + 1]
                sc = jnp.where(mask_c, s[:, hh * WINDOW:(hh + 1) * WINDOW], NEG)
                sn = jnp.where(vis_n & (half_n == hh), s_n, NEG)
                m = jnp.maximum(jnp.maximum(jnp.max(sc, axis=-1, keepdims=True),
                                            jnp.max(sn, axis=-1, keepdims=True)), sk)
                pc = jnp.exp(sc - m)
                ph = jnp.exp(sn - m)
                den = (jnp.sum(pc, axis=-1, keepdims=True) + jnp.sum(ph, axis=-1, keepdims=True)
                       + jnp.exp(sk - m))
                pcs.append(pc / den)
                pn = pn + ph / den
            p = jnp.concatenate(pcs + [pn], axis=1).astype(BF16)
            yb_ref[row0:row0 + 8, :] = _dot(p, vv)


def _attn_sample_call(q, kc, vc, kn, vn, sink_rows):
    bs = ATT_SEQS
    cache_spec = pl.BlockSpec((bs, WINDOW, KV_W), lambda i: (i, 0, 0))
    new_spec = pl.BlockSpec((bs * DEC_SEQ, KV_W), lambda i: (i, 0))
    q_spec = pl.BlockSpec((bs * 16, LANES), lambda i: (i, 0))
    return pl.pallas_call(
        _attn_sample_kernel,
        grid=(DEC_BATCH // bs,),
        in_specs=[q_spec, cache_spec, cache_spec, new_spec, new_spec,
                  pl.BlockSpec((16, 2 * LANES), lambda i: (0, 0))],
        out_specs=[q_spec, cache_spec, cache_spec],
        out_shape=[
            jax.ShapeDtypeStruct((DEC_BATCH * 16, LANES), F32),
            jax.ShapeDtypeStruct((DEC_BATCH, WINDOW, KV_W), F32),
            jax.ShapeDtypeStruct((DEC_BATCH, WINDOW, KV_W), F32),
        ],
        compiler_params=_cparams(("arbitrary",)),
        name="attn_sample",
    )(q, kc, vc, kn, vn, sink_rows)


def _rope_tables(pos):
    inv = ROPE_THETA ** (-jnp.arange(0, ROT_DIM, 2, dtype=F32) / ROT_DIM)
    ang = pos.astype(F32)[:, None] * inv[None, :]
    cos, sin = jnp.cos(ang), jnp.sin(ang)
    n = pos.shape[0]
    half = ROT_DIM // 2
    rest = HEAD_DIM - ROT_DIM
    z8 = jnp.zeros((n, half), F32)
    zr = jnp.zeros((n, rest), F32)
    c64 = jnp.concatenate([cos, cos, jnp.ones((n, rest), F32)], axis=1)
    sa64 = jnp.concatenate([z8, sin, zr], axis=1)
    sb64 = jnp.concatenate([-sin, z8, zr], axis=1)
    return tuple(jnp.tile(t, (1, LANES // HEAD_DIM)) for t in (c64, sa64, sb64))


def _ssm_params(a_re, a_im, log_dt, b_re, b_im, c_re, c_im):
    a_re, a_im = a_re.astype(F32), a_im.astype(F32)
    dt = jnp.exp(log_dt.astype(F32))[:, None]
    mag = jnp.exp(a_re * dt)
    abar_re, abar_im = mag * jnp.cos(a_im * dt), mag * jnp.sin(a_im * dt)
    inv = 1.0 / (a_re * a_re + a_im * a_im)
    f_re = (((abar_re - 1.0) * a_re + abar_im * a_im) * inv)[..., None]
    f_im = ((abar_im * a_re - (abar_re - 1.0) * a_im) * inv)[..., None]
    b_re, b_im = b_re.astype(F32), b_im.astype(F32)
    bbar_re = f_re * b_re - f_im * b_im
    bbar_im = f_re * b_im + f_im * b_re
    eye = jnp.eye(SSM_GROUPS, dtype=F32)
    half = N_STATE // 2

    def diag_in(m):
        t = jnp.einsum("gpc,gh->gchp", m, eye).reshape(W_C, N_STATE)
        return jnp.stack([t[0:256, 0:half], t[256:512, half:N_STATE]]).astype(BF16)

    def diag_out(m):
        t = jnp.einsum("gcp,gh->gphc", m, eye).reshape(N_STATE, W_C)
        return jnp.stack([t[0:half, 0:256], t[half:N_STATE, 256:512]]).astype(BF16)

    return dict(
        a_re=abar_re.reshape(1, N_STATE), a_im=abar_im.reshape(1, N_STATE),
        b_re=diag_in(bbar_re), b_im=diag_in(bbar_im),
        c_re=diag_out(c_re.astype(F32)), c_imn=diag_out(-c_im.astype(F32)),
    )


def _sink_rows_prompt(sinks):
    return jnp.repeat(sinks.astype(F32).reshape(N_HEADS // 2, 2), LANES, axis=1)


def _sink_rows_sample(sinks):
    pairs = _sink_rows_prompt(sinks).reshape(KV_HEADS, 1, 2, 2 * LANES)
    return jnp.broadcast_to(pairs, (KV_HEADS, DEC_SEQ, 2, 2 * LANES)).reshape(16, 2 * LANES)


def kernel(x_prompt, x_sample, cache_win_k, cache_win_v, state_pool, state_ssm_re, state_ssm_im, c_prompt, c_sample, norm1_g, norm2_g, w_ada, b_ada, w_in, pool_w, pool_scale, attn_sinks, ssm_a_re, ssm_a_im, ssm_log_dt, ssm_b_re, ssm_b_im, ssm_c_re, ssm_c_im, ssm_d, w_glu, w_branch_a, w_branch_b, w_branch_c, w_out, w_ffn_in, w_ffn_out, final_norm_g):
    mod_all = _ada_call(jnp.concatenate([c_prompt, c_sample], axis=0), w_ada, b_ada)
    tabs_p = _rope_tables(jnp.arange(SEQ, dtype=jnp.int32))
    tabs_s = _rope_tables(PAST_LEN + jnp.arange(DEC_SEQ, dtype=jnp.int32))
    g_final = final_norm_g.reshape(1, D_MODEL)
    zeros_state = jnp.zeros((BATCH, N_STATE), F32)

    hp = x_prompt
    hs = x_sample.reshape(DEC_BATCH, DEC_SEQ * D_MODEL)
    outs = [[] for _ in range(10)]
    for l in range(DEPTH):
        mod_p = mod_all[l, :BATCH].reshape(BATCH, 1, 6 * D_MODEL)
        mod_s = mod_all[l, BATCH:]
        g1 = norm1_g[l].reshape(1, D_MODEL)
        g2 = norm2_g[l].reshape(1, D_MODEL)
        w_mix = w_in[l, :, :MIX_COLS].astype(BF16)
        wts = dict(w_gate=w_in[l, :, MIX_COLS:].astype(BF16), w_a=w_branch_a[l].astype(BF16),
                   w_b=w_branch_b[l].astype(BF16), w_c=w_branch_c[l].astype(BF16), w_out=w_out[l].astype(BF16))
        pw = pool_w[l].astype(BF16)
        ps = pool_scale[l].reshape(1, W_A)
        sp = _ssm_params(ssm_a_re[l], ssm_a_im[l], ssm_log_dt[l], ssm_b_re[l], ssm_b_im[l], ssm_c_re[l], ssm_c_im[l])
        d_skip = ssm_d[l].reshape(1, W_C)
        wglu = w_glu[l].astype(BF16)
        wfi = w_ffn_in[l].astype(BF16)
        wfo = w_ffn_out[l].astype(BF16)
        last = l == DEPTH - 1

        ya, q, kv, u, kwin, vwin, pool_new = _proj_prompt_call(hp, mod_p, g1, w_mix, pw, ps, tabs_p)
        yb = _attn_prompt_call(q, kv, attn_sinks[l].astype(F32))
        yc, hre, him = _ssm_call(u.reshape(BATCH, SEQ, W_C), zeros_state, zeros_state, sp, d_skip, wglu,
                                 BATCH, SSM_TC, True, "ssm_prompt")
        hp = _merge_call(False, hp, mod_p, g1, ya, yb, yc.reshape(BATCH * SEQ, W_C), wts)
        hp = _ffn_call(False, last, hp, mod_p, g2, wfi, wfo, g_final)
        for k, v in zip(range(5), (kwin.reshape(BATCH, WINDOW, KV_HEADS, HEAD_DIM),
                                   vwin.reshape(BATCH, WINDOW, KV_HEADS, HEAD_DIM), pool_new,
                                   hre.reshape(BATCH, SSM_GROUPS, SSM_STATE),
                                   him.reshape(BATCH, SSM_GROUPS, SSM_STATE))):
            outs[k].append(v)

        ya, q, kn, vn, u, pool_new = _proj_sample_call(
            hs, mod_s, g1, w_mix, pw, ps, tabs_s, state_pool[l].reshape(DEC_BATCH, POOL_BUF * W_A))
        yb, k_new, v_new = _attn_sample_call(
            q, cache_win_k[l].reshape(DEC_BATCH, WINDOW, KV_W), cache_win_v[l].reshape(DEC_BATCH, WINDOW, KV_W),
            kn, vn, _sink_rows_sample(attn_sinks[l]))
        yc, hre, him = _ssm_call(u, state_ssm_re[l].reshape(DEC_BATCH, N_STATE),
                                 state_ssm_im[l].reshape(DEC_BATCH, N_STATE), sp, d_skip, wglu,
                                 DEC_BATCH, DEC_SEQ, False, "ssm_sample")
        hs = _merge_call(True, hs, mod_s, g1, ya, yb, yc, wts)
        hs = _ffn_call(True, last, hs, mod_s, g2, wfi, wfo, g_final)
        for k, v in zip(range(5, 10), (k_new.reshape(DEC_BATCH, WINDOW, KV_HEADS, HEAD_DIM),
                                       v_new.reshape(DEC_BATCH, WINDOW, KV_HEADS, HEAD_DIM),
                                       pool_new.reshape(DEC_BATCH, POOL_BUF, W_A),
                                       hre.reshape(DEC_BATCH, SSM_GROUPS, SSM_STATE),
                                       him.reshape(DEC_BATCH, SSM_GROUPS, SSM_STATE))):
            outs[k].append(v)

    return (hp, hs.reshape(DEC_BATCH, DEC_SEQ, D_MODEL), *[jnp.stack(o) for o in outs])
```

```python
import functools
import math

import jax
import jax.numpy as jnp
from jax import lax
from jax.experimental import pallas as pl
from jax.experimental.pallas import tpu as pltpu

F32 = jnp.float32
BF16 = jnp.bfloat16

D_MODEL = 1024
BATCH = 8
SEQ = 2048
DEPTH = 4
DEC_BATCH = 128
DEC_SEQ = 4
PAST_LEN = 8192

W_A = 512
POOL_WINDOWS = (2, 4, 8, 16)
POOL_GROUP = 128
POOL_BUF = 15
POOL_HIST = 32
HEAD_DIM = 64
N_HEADS = 8
KV_HEADS = 2
W_B = 512
KV_W = 128
WINDOW = 128
ROT_DIM = 16
ROPE_THETA = 500000.0
W_C = 512
SSM_CH = 16
SSM_GROUPS = 32
SSM_STATE = 64
N_STATE = SSM_GROUPS * SSM_STATE
D_FF = 2816
EPS = 1e-6
MIX_COLS = W_A + W_B + 2 * KV_W + W_C

LANES = 128
SUBLANES = 8
VMEM_LIMIT = 56 * 1024 * 1024

TM = 512
NT = SEQ // TM
ATT_BLK = WINDOW
ATT_SUB = 2
SSM_TC = 64
SCAN_CW = 512
NEG = -1e30
FF_CHUNKS = ((0, 1024), (1024, 1024), (2048, 768))


def _cparams(sem):
    return pltpu.CompilerParams(dimension_semantics=sem, vmem_limit_bytes=VMEM_LIMIT)


def _dot(a, b):
    return jnp.dot(a, b, preferred_element_type=F32)


def _norm_mod(x, g, sc, sh):
    r = lax.rsqrt(jnp.mean(x * x, axis=-1, keepdims=True) + EPS)
    return (x * r) * g * (1.0 + sc) + sh


def _rope(x, cos, sa, sb):
    return x * cos + pltpu.roll(x, 8, axis=1) * sa + pltpu.roll(x, LANES - 8, axis=1) * sb


def _ada_kernel(c_ref, w_ref, b_ref, o_ref):
    c = c_ref[...]
    a = (c * jax.nn.sigmoid(c)).astype(BF16)
    o_ref[...] = _dot(a, w_ref[...].astype(BF16)) + b_ref[...]


def _ada_call(c_all, w_ada, b_ada):
    n = c_all.shape[0]
    cb = 1536
    return pl.pallas_call(
        _ada_kernel,
        grid=(DEPTH, 6 * D_MODEL // cb),
        in_specs=[
            pl.BlockSpec((n, D_MODEL), lambda l, j: (0, 0)),
            pl.BlockSpec((None, D_MODEL, cb), lambda l, j: (l, 0, j)),
            pl.BlockSpec((None, 1, cb), lambda l, j: (l, 0, j)),
        ],
        out_specs=pl.BlockSpec((None, n, cb), lambda l, j: (l, 0, j)),
        out_shape=jax.ShapeDtypeStruct((DEPTH, n, 6 * D_MODEL), F32),
        compiler_params=_cparams(("arbitrary", "arbitrary")),
        name="ada",
    )(c_all, w_ada, b_ada.reshape(DEPTH, 1, 6 * D_MODEL))


def _proj_prompt_kernel(x_ref, sh_ref, sc_ref, g_ref, w_ref, pw_ref, ps_ref, cos_ref, sa_ref, sb_ref,
                        ya_ref, q_ref, kv_ref, u_ref, kwin_ref, vwin_ref, pool_ref, xe_ref, s2_ref, s4_ref, s8_ref):
    t = pl.program_id(1)
    tm = x_ref.shape[0]
    hist = POOL_HIST
    end = hist + tm
    h = _norm_mod(x_ref[...], g_ref[...], sc_ref[...], sh_ref[...]).astype(BF16)
    z = _dot(h, w_ref[...])
    xa = z[:, 0:W_A]

    @pl.when(t == 0)
    def _():
        xe_ref[0:hist, :] = jnp.zeros((hist, W_A), F32)

    xe_ref[hist:end, :] = xa
    g1, g2, g3 = POOL_GROUP, 2 * POOL_GROUP, 3 * POOL_GROUP
    s2_ref[8:end, :] = xe_ref[8:end, :] + xe_ref[7:end - 1, :]
    s4_ref[16:end, :] = s2_ref[16:end, g1:] + s2_ref[14:end - 2, g1:]
    s8_ref[24:end, :] = s4_ref[24:end, g1:] + s4_ref[20:end - 4, g1:]
    sums = (s2_ref[hist:end, 0:g1], s4_ref[hist:end, 0:g1], s8_ref[hist:end, 0:g1],
            s8_ref[hist:end, g1:g2] + s8_ref[hist - 8:end - 8, g1:g2])
    pos = t * tm + lax.broadcasted_iota(jnp.int32, (tm, 1), 0)
    ds = []
    for g, w in enumerate(POOL_WINDOWS):
        cnt = jnp.minimum(pos + 1, w).astype(F32)
        ds.append((sums[g] / cnt - xa[:, g * POOL_GROUP:(g + 1) * POOL_GROUP]).astype(BF16))
    for k in range(2):
        y = _dot(jnp.concatenate(ds[2 * k:2 * k + 2], axis=1), pw_ref[k]) * ps_ref[:, k * g2:(k + 1) * g2]
        ya_ref[:, k * g2:(k + 1) * g2] = y.astype(BF16)

    @pl.when(t == pl.num_programs(1) - 1)
    def _():
        pool_ref[...] = xe_ref[end - POOL_BUF:end, :]

    xe_ref[0:hist, :] = xe_ref[tm:end, :]

    cos = cos_ref[...]
    sa = sa_ref[...]
    sb = sb_ref[...]
    for c in range(W_B // LANES):
        qc = z[:, W_A + c * LANES:W_A + (c + 1) * LANES]
        q_ref[:, c * LANES:(c + 1) * LANES] = (_rope(qc, cos, sa, sb) * (HEAD_DIM ** -0.5)).astype(BF16)
    kz = _rope(z[:, W_A + W_B:W_A + W_B + KV_W], cos, sa, sb)
    vz = z[:, W_A + W_B + KV_W:W_A + W_B + 2 * KV_W]
    kv_ref[:, 0:128] = kz.astype(BF16)
    kv_ref[:, 128:256] = pltpu.roll(kz, HEAD_DIM, axis=1).astype(BF16)
    kv_ref[:, 256:384] = vz.astype(BF16)
    kv_ref[:, 384:512] = pltpu.roll(vz, HEAD_DIM, axis=1).astype(BF16)

    @pl.when(t == pl.num_programs(1) - 1)
    def _():
        kwin_ref[...] = kz[tm - WINDOW:, :]
        vwin_ref[...] = vz[tm - WINDOW:, :]

    u_ref[...] = z[:, W_A + W_B + 2 * KV_W:MIX_COLS]


def _proj_prompt_call(x, mod, g1, w_mix, pool_w, pool_scale, rope_tabs):
    rows = BATCH * SEQ
    row_blk = lambda b, t: (b * NT + t, 0)
    const2 = lambda b, t: (0, 0)
    mod_spec = lambda j: pl.BlockSpec((None, 1, D_MODEL), lambda b, t: (b, 0, j))
    tab_spec = pl.BlockSpec((TM, LANES), lambda b, t: (t, 0))
    return pl.pallas_call(
        _proj_prompt_kernel,
        grid=(BATCH, NT),
        in_specs=[
            pl.BlockSpec((None, TM, D_MODEL), lambda b, t: (b, t, 0)),
            mod_spec(0), mod_spec(1),
            pl.BlockSpec((1, D_MODEL), const2),
            pl.BlockSpec((D_MODEL, MIX_COLS), const2),
            pl.BlockSpec((2, 2 * POOL_GROUP, 2 * POOL_GROUP), lambda b, t: (0, 0, 0)),
            pl.BlockSpec((1, W_A), const2),
            tab_spec, tab_spec, tab_spec,
        ],
        out_specs=[
            pl.BlockSpec((TM, W_A), row_blk),
            pl.BlockSpec((TM, W_B), row_blk),
            pl.BlockSpec((TM, 4 * KV_W), row_blk),
            pl.BlockSpec((TM, W_C), row_blk),
            pl.BlockSpec((None, WINDOW, KV_W), lambda b, t: (b, 0, 0)),
            pl.BlockSpec((None, WINDOW, KV_W), lambda b, t: (b, 0, 0)),
            pl.BlockSpec((None, POOL_BUF, W_A), lambda b, t: (b, 0, 0)),
        ],
        out_shape=[
            jax.ShapeDtypeStruct((rows, W_A), BF16),
            jax.ShapeDtypeStruct((rows, W_B), BF16),
            jax.ShapeDtypeStruct((rows, 4 * KV_W), BF16),
            jax.ShapeDtypeStruct((rows, W_C), F32),
            jax.ShapeDtypeStruct((BATCH, WINDOW, KV_W), F32),
            jax.ShapeDtypeStruct((BATCH, WINDOW, KV_W), F32),
            jax.ShapeDtypeStruct((BATCH, POOL_BUF, W_A), F32),
        ],
        scratch_shapes=[pltpu.VMEM((TM + POOL_HIST, W_A), F32),
                        pltpu.VMEM((TM + POOL_HIST, W_A), F32),
                        pltpu.VMEM((TM + POOL_HIST, W_A - POOL_GROUP), F32),
                        pltpu.VMEM((TM + POOL_HIST, W_A - 2 * POOL_GROUP), F32)],
        compiler_params=_cparams(("arbitrary", "arbitrary")),
        name="proj_prompt",
    )(x, mod, mod, g1, w_mix, pool_w, pool_scale, *rope_tabs)


def _head_pair_operand(lo_mask, a, b):
    zero = jnp.zeros_like(a)
    return jnp.concatenate([jnp.where(lo_mask, a, zero), jnp.where(lo_mask, zero, b)], axis=0)


def _attn_window_block(q_ref, yb_ref, sink_ref, r0, kv_prev, kv_cur, prev_bias):
    n = ATT_BLK
    lo = lax.broadcasted_iota(jnp.int32, (n, LANES), 1) < HEAD_DIM
    rows = jnp.bitwise_and(lax.broadcasted_iota(jnp.int32, (2 * n, LANES), 0), n - 1)
    from_prev = lax.broadcasted_iota(jnp.int32, (2 * n, LANES), 1) > rows
    lo2 = lax.broadcasted_iota(jnp.int32, (2 * n, LANES), 1) < HEAD_DIM
    top = lax.broadcasted_iota(jnp.int32, (2 * n, 1), 0) < n
    zero = jnp.zeros((n, LANES), BF16)
    ones_a = jnp.where(lo, 1.0, 0.0).astype(BF16)
    ones_b = jnp.where(lo, 0.0, 1.0).astype(BF16)
    nt_dims = (((1,), (1,)), ((), ()))
    for g in range(KV_HEADS):
        ca, cb = (0, LANES) if g == 0 else (LANES, 0)
        kk = jnp.concatenate([jnp.where(lo, kv_prev[:, ca:ca + LANES], zero),
                              jnp.where(lo, kv_cur[:, ca:ca + LANES], zero),
                              jnp.where(lo, zero, kv_prev[:, cb:cb + LANES]),
                              jnp.where(lo, zero, kv_cur[:, cb:cb + LANES])], axis=0)
        va, vb = 2 * LANES + ca, 2 * LANES + cb
        vv = jnp.concatenate([
            jnp.concatenate([jnp.where(lo, kv_prev[:, va:va + LANES], zero), ones_a], axis=1),
            jnp.concatenate([jnp.where(lo, kv_cur[:, va:va + LANES], zero), ones_a], axis=1),
            jnp.concatenate([jnp.where(lo, zero, kv_prev[:, vb:vb + LANES]), ones_b], axis=1),
            jnp.concatenate([jnp.where(lo, zero, kv_cur[:, vb:vb + LANES]), ones_b], axis=1)], axis=0)
        qs = jnp.concatenate([q_ref[pl.ds(r0, n), (2 * g) * LANES:(2 * g + 1) * LANES],
                              q_ref[pl.ds(r0, n), (2 * g + 1) * LANES:(2 * g + 2) * LANES]], axis=0)
        s = lax.dot_general(qs, kk, nt_dims, preferred_element_type=F32)
        ps, es = [], []
        for hh in range(2):
            s_prev = s[:, hh * 2 * n:hh * 2 * n + n]
            if prev_bias is not None:
                s_prev = s_prev + prev_bias
            logits = jnp.where(from_prev, s_prev, s[:, hh * 2 * n + n:(hh + 1) * 2 * n])
            sk = jnp.where(top, sink_ref[4 * g + hh], sink_ref[4 * g + 2 + hh])
            m = jnp.maximum(jnp.max(logits, axis=-1, keepdims=True), sk)
            p = jnp.exp(logits - m)
            ps += [jnp.where(from_prev, p, 0.0), jnp.where(from_prev, 0.0, p)]
            es.append(jnp.exp(sk - m))
        r = _dot(jnp.concatenate(ps, axis=1).astype(BF16), vv)
        o = r[:, 0:LANES] / (r[:, LANES:2 * LANES] + jnp.where(lo2, es[0], es[1]))
        yb_ref[pl.ds(r0, n), (2 * g) * LANES:(2 * g + 1) * LANES] = o[0:n].astype(BF16)
        yb_ref[pl.ds(r0, n), (2 * g + 1) * LANES:(2 * g + 2) * LANES] = o[n:2 * n].astype(BF16)


def _attn_prompt_kernel(sink_ref, q_ref, kvc_ref, kvp_ref, yb_ref):
    i = pl.program_id(1)
    first_bias = jnp.where(i > 0, 0.0, NEG).astype(F32)
    for sub in range(ATT_SUB):
        r0 = sub * ATT_BLK
        kv_cur = kvc_ref[r0:r0 + ATT_BLK, :]
        kv_prev = kvp_ref[...] if sub == 0 else kvc_ref[r0 - ATT_BLK:r0, :]
        _attn_window_block(q_ref, yb_ref, sink_ref, r0, kv_prev, kv_cur, first_bias if sub == 0 else None)


def _attn_prompt_call(q, kv, sinks):
    rows = ATT_SUB * ATT_BLK
    nb = SEQ // rows
    return pl.pallas_call(
        _attn_prompt_kernel,
        grid=(BATCH, nb),
        in_specs=[
            pl.BlockSpec(memory_space=pltpu.SMEM),
            pl.BlockSpec((rows, W_B), lambda b, i: (b * nb + i, 0)),
            pl.BlockSpec((rows, 4 * KV_W), lambda b, i: (b * nb + i, 0)),
            pl.BlockSpec((ATT_BLK, 4 * KV_W), lambda b, i: ((b * nb + i) * ATT_SUB - jnp.minimum(i, 1), 0)),
        ],
        out_specs=pl.BlockSpec((rows, W_B), lambda b, i: (b * nb + i, 0)),
        out_shape=jax.ShapeDtypeStruct((BATCH * SEQ, W_B), BF16),
        compiler_params=_cparams(("arbitrary", "arbitrary")),
        name="attn_prompt",
    )(sinks, q, kv, kv)


def _ssm_kernel(rows_per_step, seq_major, u_ref, h0re_ref, h0im_ref, are_ref, aim_ref, bre_ref, bim_ref, cre_ref,
                cimn_ref, d_ref, wglu_ref, yc_ref, hre_out, him_out, dre, dim_, hre, him, *slabs):
    i = pl.program_id(0)
    r = rows_per_step
    half = N_STATE // 2
    nslab = W_C // LANES

    @pl.when(i == 0)
    def _():
        hre[...] = h0re_ref[...]
        him[...] = h0im_ref[...]

    if seq_major:
        (perm,) = slabs
        tc = u_ref.shape[1]
        for b in range(r):
            for c in range(nslab):
                perm[c, pl.ds(b, tc, stride=r), :] = u_ref[b, :, c * LANES:(c + 1) * LANES]
        ub = jnp.concatenate([perm[c] for c in range(nslab)], axis=1).astype(BF16)
    else:
        tc = u_ref.shape[0] // r
        ub = u_ref[...].astype(BF16)
    for kt in range(2):
        uk = ub[:, kt * 256:(kt + 1) * 256]
        dre[:, kt * half:(kt + 1) * half] = _dot(uk, bre_ref[kt])
        dim_[:, kt * half:(kt + 1) * half] = _dot(uk, bim_ref[kt])

    for rt in range(r // SUBLANES):
        for c in range(N_STATE // SCAN_CW):
            cols = slice(c * SCAN_CW, (c + 1) * SCAN_CW)
            rsl = slice(rt * SUBLANES, (rt + 1) * SUBLANES)
            ar = jnp.broadcast_to(are_ref[:, cols], (SUBLANES, SCAN_CW))
            ai = jnp.broadcast_to(aim_ref[:, cols], (SUBLANES, SCAN_CW))

            def step(t, carry, cols=cols, rt=rt, ar=ar, ai=ai):
                pr, pi = carry
                row = pl.multiple_of(t * r + rt * SUBLANES, SUBLANES)
                nr = ar * pr - ai * pi + dre[pl.ds(row, SUBLANES), cols]
                ni = ar * pi + ai * pr + dim_[pl.ds(row, SUBLANES), cols]
                dre[pl.ds(row, SUBLANES), cols] = nr
                dim_[pl.ds(row, SUBLANES), cols] = ni
                return nr, ni

            fr, fi = lax.fori_loop(0, tc, step, (hre[rsl, cols], him[rsl, cols]), unroll=min(tc, 4))
            hre[rsl, cols] = fr
            him[rsl, cols] = fi

    @pl.when(i == pl.num_programs(0) - 1)
    def _():
        hre_out[...] = hre[...]
        him_out[...] = him[...]

    ys = []
    for nt in range(2):
        sl = slice(nt * half, (nt + 1) * half)
        ys.append(_dot(dre[:, sl].astype(BF16), cre_ref[nt]) + _dot(dim_[:, sl].astype(BF16), cimn_ref[nt]))
    if seq_major:
        for c in range(nslab):
            perm[c] = ys[c // 2][:, (c % 2) * LANES:(c % 2 + 1) * LANES]
        y = jnp.concatenate(
            [jnp.concatenate([perm[c, pl.ds(b, tc, stride=r), :] for c in range(nslab)], axis=1) for b in range(r)],
            axis=0)
        u = u_ref[...].reshape(r * tc, W_C)
    else:
        y = jnp.concatenate(ys, axis=1)
        u = u_ref[...]
    y = jax.nn.gelu(y + d_ref[...] * u)
    yc = (y * jax.nn.sigmoid(_dot(y.astype(BF16), wglu_ref[...]))).astype(BF16)
    yc_ref[...] = yc.reshape(yc_ref.shape)


def _ssm_call(u, h0re, h0im, sp, d, w_glu, rows_per_step, steps_per_block, seq_major, name):
    blk = rows_per_step * steps_per_block
    const2 = lambda i: (0, 0)
    const3 = lambda i: (0, 0, 0)
    half = N_STATE // 2
    if seq_major:
        nsteps = u.shape[1] // steps_per_block
        io_spec = pl.BlockSpec((rows_per_step, steps_per_block, W_C), lambda i: (0, i, 0))
        scratch = [pltpu.VMEM((W_C // LANES, blk, LANES), F32)]
    else:
        nsteps = u.shape[0] // blk
        io_spec = pl.BlockSpec((blk, W_C), lambda i: (i, 0))
        scratch = []
    return pl.pallas_call(
        functools.partial(_ssm_kernel, rows_per_step, seq_major),
        grid=(nsteps,),
        in_specs=[
            io_spec,
            pl.BlockSpec((rows_per_step, N_STATE), const2),
            pl.BlockSpec((rows_per_step, N_STATE), const2),
            pl.BlockSpec((1, N_STATE), const2),
            pl.BlockSpec((1, N_STATE), const2),
            pl.BlockSpec((2, 256, half), const3),
            pl.BlockSpec((2, 256, half), const3),
            pl.BlockSpec((2, half, 256), const3),
            pl.BlockSpec((2, half, 256), const3),
            pl.BlockSpec((1, W_C), const2),
            pl.BlockSpec((W_C, W_C), const2),
        ],
        out_specs=[
            io_spec,
            pl.BlockSpec((rows_per_step, N_STATE), const2),
            pl.BlockSpec((rows_per_step, N_STATE), const2),
        ],
        out_shape=[
            jax.ShapeDtypeStruct(u.shape, BF16),
            jax.ShapeDtypeStruct((rows_per_step, N_STATE), F32),
            jax.ShapeDtypeStruct((rows_per_step, N_STATE), F32),
        ],
        scratch_shapes=[
            pltpu.VMEM((blk, N_STATE), F32),
            pltpu.VMEM((blk, N_STATE), F32),
            pltpu.VMEM((rows_per_step, N_STATE), F32),
            pltpu.VMEM((rows_per_step, N_STATE), F32),
        ] + scratch,
        compiler_params=_cparams(("arbitrary",)),
        name=name,
    )(u, h0re, h0im, sp["a_re"], sp["a_im"], sp["b_re"], sp["b_im"], sp["c_re"], sp["c_imn"], d, w_glu)


def _merge_kernel(sample, x_ref, sh_ref, sc_ref, gt_ref, g_ref, ya_ref, yb_ref, yc_ref,
                  wg_ref, wa_ref, wb_ref, wc_ref, wo_ref, o_ref):
    x = x_ref[...]
    h = _norm_mod(x, g_ref[...], sc_ref[...], sh_ref[...]).astype(BF16)
    if sample:
        t = pl.program_id(1)
        parts = []
        for j in range(N_HEADS // 2):
            start = (j // 2) * 8 + t * 2 + (j % 2)
            parts.append(yb_ref[pl.ds(start, DEC_BATCH, stride=16), :])
        yb = jnp.concatenate(parts, axis=1).astype(BF16)
    else:
        yb = yb_ref[...]
    merged = None
    for k, (y, w_ref) in enumerate(((ya_ref[...], wa_ref), (yb, wb_ref), (yc_ref[...], wc_ref))):
        gate = jax.nn.sigmoid(_dot(h, wg_ref[:, k * D_MODEL:(k + 1) * D_MODEL]))
        term = gate * _dot(y, w_ref[...])
        merged = term if merged is None else merged + term
    o_ref[...] = x + gt_ref[...] * _dot(merged.astype(BF16), wo_ref[...])


def _row_specs(sample):
    if sample:
        tm = DEC_BATCH
        grid = (1, DEC_SEQ)
        x_spec = pl.BlockSpec((tm, D_MODEL), lambda b, t: (0, t))
        mod_spec = lambda j: pl.BlockSpec((tm, D_MODEL), lambda b, t: (0, j))
        row_spec = lambda cols: pl.BlockSpec((tm, cols), lambda b, t: (t, 0))
        yc_spec = row_spec(W_C)
    else:
        tm = TM
        grid = (BATCH, NT)
        x_spec = pl.BlockSpec((None, tm, D_MODEL), lambda b, t: (b, t, 0))
        mod_spec = lambda j: pl.BlockSpec((None, 1, D_MODEL), lambda b, t: (b, 0, j))
        row_spec = lambda cols: pl.BlockSpec((tm, cols), lambda b, t: (b * NT + t, 0))
        yc_spec = row_spec(W_C)
    return tm, grid, x_spec, mod_spec, row_spec, yc_spec


def _full_spec(shape):
    nd = len(shape)
    return pl.BlockSpec(shape, lambda b, t: (0,) * nd)


def _merge_call(sample, x, mod, g1, ya, yb, yc, wts):
    tm, grid, x_spec, mod_spec, row_spec, yc_spec = _row_specs(sample)
    yb_spec = _full_spec(yb.shape) if sample else row_spec(W_B)
    return pl.pallas_call(
        functools.partial(_merge_kernel, sample),
        grid=grid,
        in_specs=[
            x_spec, mod_spec(0), mod_spec(1), mod_spec(2),
            _full_spec((1, D_MODEL)),
            row_spec(W_A), yb_spec, yc_spec,
            _full_spec((D_MODEL, 3 * D_MODEL)),
            _full_spec((W_A, D_MODEL)), _full_spec((W_B, D_MODEL)), _full_spec((W_C, D_MODEL)),
            _full_spec((D_MODEL, D_MODEL)),
        ],
        out_specs=x_spec,
        out_shape=jax.ShapeDtypeStruct(x.shape, F32),
        compiler_params=_cparams(("arbitrary", "arbitrary")),
        name="merge_sample" if sample else "merge_prompt",
    )(x, mod, mod, mod, g1, ya, yb, yc, wts["w_gate"], wts["w_a"], wts["w_b"], wts["w_c"], wts["w_out"])


def _ffn_kernel(final, x_ref, sh_ref, sc_ref, gt_ref, g_ref, wi_ref, wo_ref, gf_ref, o_ref):
    x = x_ref[...]
    h = _norm_mod(x, g_ref[...], sc_ref[...], sh_ref[...]).astype(BF16)
    acc = None
    for lo, n in FF_CHUNKS:
        a = _dot(h, wi_ref[:, lo:lo + n])
        b = _dot(h, wi_ref[:, D_FF + lo:D_FF + lo + n])
        act = ((a * jax.nn.sigmoid(a)) * b).astype(BF16)
        part = _dot(act, wo_ref[lo:lo + n, :])
        acc = part if acc is None else acc + part
    y = x + gt_ref[...] * acc
    if final:
        r = lax.rsqrt(jnp.mean(y * y, axis=-1, keepdims=True) + EPS)
        y = (y * r) * gf_ref[...]
    o_ref[...] = y


def _ffn_call(sample, final, x, mod, g2, w_ffn_in, w_ffn_out, g_final):
    tm, grid, x_spec, mod_spec, _, _ = _row_specs(sample)
    single = pl.Buffered(1)
    return pl.pallas_call(
        functools.partial(_ffn_kernel, final),
        grid=grid,
        in_specs=[
            x_spec, mod_spec(3), mod_spec(4), mod_spec(5),
            _full_spec((1, D_MODEL)),
            pl.BlockSpec((D_MODEL, 2 * D_FF), lambda b, t: (0, 0), pipeline_mode=single),
            pl.BlockSpec((D_FF, D_MODEL), lambda b, t: (0, 0), pipeline_mode=single),
            _full_spec((1, D_MODEL)),
        ],
        out_specs=x_spec,
        out_shape=jax.ShapeDtypeStruct(x.shape, F32),
        compiler_params=_cparams(("arbitrary", "arbitrary")),
        name="ffn_sample" if sample else "ffn_prompt",
    )(x, mod, mod, mod, g2, w_ffn_in, w_ffn_out, g_final)


def _proj_sample_kernel(x_ref, mod_ref, g_ref, w_ref, pw_ref, ps_ref, cos_ref, sa_ref, sb_ref, pool_prev_ref,
                        ya_ref, q_ref, kn_ref, vn_ref, u_ref, pool_ref):
    nb = DEC_BATCH
    sh = mod_ref[:, 0:D_MODEL]
    sc = mod_ref[:, D_MODEL:2 * D_MODEL]
    h = jnp.concatenate(
        [_norm_mod(x_ref[:, t * D_MODEL:(t + 1) * D_MODEL], g_ref[...], sc, sh) for t in range(DEC_SEQ)],
        axis=0).astype(BF16)

    z = _dot(h, w_ref[...])
    xa = z[:, 0:W_A]
    xe = [pool_prev_ref[:, j * W_A:(j + 1) * W_A] for j in range(POOL_BUF)]
    xe += [xa[t * nb:(t + 1) * nb, :] for t in range(DEC_SEQ)]
    for j in range(POOL_BUF):
        pool_ref[:, j * W_A:(j + 1) * W_A] = xe[DEC_SEQ + j]
    dgs = []
    for g, w in enumerate(POOL_WINDOWS):
        lo = g * POOL_GROUP
        ds = []
        for t in range(DEC_SEQ):
            s = xe[POOL_BUF + t][:, lo:lo + POOL_GROUP]
            for j in range(1, w):
                s = s + xe[POOL_BUF + t - j][:, lo:lo + POOL_GROUP]
            ds.append(s / float(w) - xe[POOL_BUF + t][:, lo:lo + POOL_GROUP])
        dgs.append(jnp.concatenate(ds, axis=0).astype(BF16))
    for k in range(2):
        cols = slice(2 * k * POOL_GROUP, 2 * (k + 1) * POOL_GROUP)
        y = _dot(jnp.concatenate(dgs[2 * k:2 * k + 2], axis=1), pw_ref[k]) * ps_ref[:, cols]
        ya_ref[:, cols] = y.astype(BF16)

    def tabs(t):
        return cos_ref[t:t + 1, :], sa_ref[t:t + 1, :], sb_ref[t:t + 1, :]

    for j in range(W_B // LANES):
        qc = z[:, W_A + j * LANES:W_A + (j + 1) * LANES]
        for t in range(DEC_SEQ):
            qt = _rope(qc[t * nb:(t + 1) * nb, :], *tabs(t)) * (HEAD_DIM ** -0.5)
            q_ref[pl.ds((j // 2) * 8 + t * 2 + (j % 2), nb, stride=16), :] = qt
    kz = z[:, W_A + W_B:W_A + W_B + KV_W]
    vz = z[:, W_A + W_B + KV_W:W_A + W_B + 2 * KV_W]
    for t in range(DEC_SEQ):
        kn_ref[pl.ds(t, nb, stride=DEC_SEQ), :] = _rope(kz[t * nb:(t + 1) * nb, :], *tabs(t))
        vn_ref[pl.ds(t, nb, stride=DEC_SEQ), :] = vz[t * nb:(t + 1) * nb, :]
    u_ref[...] = z[:, W_A + W_B + 2 * KV_W:MIX_COLS]


def _proj_sample_call(x2d, mod, g1, w_mix, pool_w, pool_scale, rope_tabs, pool_prev):
    rows = DEC_BATCH * DEC_SEQ
    full = lambda shape: pl.BlockSpec(shape, lambda i: (0,) * len(shape))
    args = (x2d, mod, g1, w_mix, pool_w, pool_scale, *rope_tabs, pool_prev)
    out_shape = [
        jax.ShapeDtypeStruct((rows, W_A), BF16),
        jax.ShapeDtypeStruct((DEC_BATCH * 16, LANES), F32),
        jax.ShapeDtypeStruct((rows, KV_W), F32),
        jax.ShapeDtypeStruct((rows, KV_W), F32),
        jax.ShapeDtypeStruct((rows, W_C), F32),
        jax.ShapeDtypeStruct((DEC_BATCH, POOL_BUF * W_A), F32),
    ]
    return pl.pallas_call(
        _proj_sample_kernel,
        grid=(1,),
        in_specs=[full(a.shape) for a in args],
        out_specs=[full(s.shape) for s in out_shape],
        out_shape=out_shape,
        compiler_params=_cparams(("arbitrary",)),
        name="proj_sample",
    )(*args)


ATT_SEQS = 8


def _attn_sample_kernel(q_ref, kc_ref, vc_ref, kn_ref, vn_ref, sink_ref, yb_ref, ko_ref, vo_ref):
    nn = 2 * SUBLANES
    nprob = KV_HEADS * ATT_SEQS
    nrow = nprob * SUBLANES
    lo_c = lax.broadcasted_iota(jnp.int32, (WINDOW, LANES), 1) < HEAD_DIM
    lo_n = lax.broadcasted_iota(jnp.int32, (SUBLANES, LANES), 1) < HEAD_DIM
    pad = jnp.zeros((SUBLANES - DEC_SEQ, LANES), F32)
    nt_dims = (((1,), (1,)), ((), ()))

    for i in range(ATT_SEQS):
        ko_ref[i, 0:WINDOW - DEC_SEQ, :] = kc_ref[i, DEC_SEQ:WINDOW, :]
        ko_ref[i, WINDOW - DEC_SEQ:WINDOW, :] = kn_ref[i * DEC_SEQ:(i + 1) * DEC_SEQ, :]
        vo_ref[i, 0:WINDOW - DEC_SEQ, :] = vc_ref[i, DEC_SEQ:WINDOW, :]
        vo_ref[i, WINDOW - DEC_SEQ:WINDOW, :] = vn_ref[i * DEC_SEQ:(i + 1) * DEC_SEQ, :]

    def pair_operands(cache, new):
        new8 = jnp.concatenate([new, pad], axis=0)
        cr, nr = pltpu.roll(cache, HEAD_DIM, axis=1), pltpu.roll(new8, HEAD_DIM, axis=1)
        out = []
        for g in range(KV_HEADS):
            (ca, cb), (na, nb) = ((cache, cr), (new8, nr)) if g == 0 else ((cr, cache), (nr, new8))
            out.append(jnp.concatenate([_head_pair_operand(lo_c, ca, cb), _head_pair_operand(lo_n, na, nb)],
                                       axis=0).astype(BF16))
        return out

    kks, vvs = [], []
    for i in range(ATT_SEQS):
        kks += pair_operands(kc_ref[i], kn_ref[i * DEC_SEQ:(i + 1) * DEC_SEQ, :])
        vvs += pair_operands(vc_ref[i], vn_ref[i * DEC_SEQ:(i + 1) * DEC_SEQ, :])

    s = jnp.concatenate(
        [lax.dot_general(q_ref[k * SUBLANES:(k + 1) * SUBLANES, :].astype(BF16), kks[k], nt_dims,
                         preferred_element_type=F32) for k in range(nprob)], axis=0)
    tq_c = jnp.right_shift(jnp.bitwise_and(lax.broadcasted_iota(jnp.int32, (nrow, WINDOW), 0), SUBLANES - 1), 1)
    mask_c = lax.broadcasted_iota(jnp.int32, (nrow, WINDOW), 1) > tq_c
    tq_n = jnp.right_shift(jnp.bitwise_and(lax.broadcasted_iota(jnp.int32, (nrow, nn), 0), SUBLANES - 1), 1)
    col_n = lax.broadcasted_iota(jnp.int32, (nrow, nn), 1)
    vis_n = jnp.bitwise_and(col_n, SUBLANES - 1) <= tq_n
    half_n = jnp.right_shift(col_n, 3)
    s_n = s[:, 2 * WINDOW:2 * WINDOW + nn]
    pcs = []
    pn = jnp.zeros((nrow, nn), F32)
    for hh in range(2):
        sk = sink_ref[:, hh * LANES:hh * LANES + 1]
        sc = jnp.where(mask_c, s[:, hh * WINDOW:(hh + 1) * WINDOW], NEG)
        sn = jnp.where(vis_n & (half_n == hh), s_n, NEG)
        m = jnp.maximum(jnp.maximum(jnp.max(sc, axis=-1, keepdims=True), jnp.max(sn, axis=-1, keepdims=True)), sk)
        pc = jnp.exp(sc - m)
        ph = jnp.exp(sn - m)
        den = jnp.sum(pc, axis=-1, keepdims=True) + jnp.sum(ph, axis=-1, keepdims=True) + jnp.exp(sk - m)
        pcs.append(pc / den)
        pn = pn + ph / den
    p = jnp.concatenate(pcs + [pn], axis=1)
    yb_ref[...] = jnp.concatenate(
        [_dot(p[k * SUBLANES:(k + 1) * SUBLANES, :].astype(BF16), vvs[k]) for k in range(nprob)], axis=0)


def _attn_sample_call(q, kc, vc, kn, vn, sink_rows):
    bs = ATT_SEQS
    cache_spec = pl.BlockSpec((bs, WINDOW, KV_W), lambda i: (i, 0, 0))
    new_spec = pl.BlockSpec((bs * DEC_SEQ, KV_W), lambda i: (i, 0))
    q_spec = pl.BlockSpec((bs * 16, LANES), lambda i: (i, 0))
    return pl.pallas_call(
        _attn_sample_kernel,
        grid=(DEC_BATCH // bs,),
        in_specs=[q_spec, cache_spec, cache_spec, new_spec, new_spec,
                  pl.BlockSpec((bs * 16, 2 * LANES), lambda i: (0, 0))],
        out_specs=[q_spec, cache_spec, cache_spec],
        out_shape=[
            jax.ShapeDtypeStruct((DEC_BATCH * 16, LANES), F32),
            jax.ShapeDtypeStruct((DEC_BATCH, WINDOW, KV_W), F32),
            jax.ShapeDtypeStruct((DEC_BATCH, WINDOW, KV_W), F32),
        ],
        compiler_params=_cparams(("arbitrary",)),
        name="attn_sample",
    )(q, kc, vc, kn, vn, sink_rows)


def _rope_tables(pos):
    inv = ROPE_THETA ** (-jnp.arange(0, ROT_DIM, 2, dtype=F32) / ROT_DIM)
    ang = pos.astype(F32)[:, None] * inv[None, :]
    cos, sin = jnp.cos(ang), jnp.sin(ang)
    n = pos.shape[0]
    half = ROT_DIM // 2
    rest = HEAD_DIM - ROT_DIM
    z8 = jnp.zeros((n, half), F32)
    zr = jnp.zeros((n, rest), F32)
    c64 = jnp.concatenate([cos, cos, jnp.ones((n, rest), F32)], axis=1)
    sa64 = jnp.concatenate([z8, sin, zr], axis=1)
    sb64 = jnp.concatenate([-sin, z8, zr], axis=1)
    return tuple(jnp.tile(t, (1, LANES // HEAD_DIM)) for t in (c64, sa64, sb64))


def _ssm_params(a_re, a_im, log_dt, b_re, b_im, c_re, c_im):
    a_re, a_im = a_re.astype(F32), a_im.astype(F32)
    dt = jnp.exp(log_dt.astype(F32))[:, None]
    mag = jnp.exp(a_re * dt)
    abar_re, abar_im = mag * jnp.cos(a_im * dt), mag * jnp.sin(a_im * dt)
    inv = 1.0 / (a_re * a_re + a_im * a_im)
    f_re = (((abar_re - 1.0) * a_re + abar_im * a_im) * inv)[..., None]
    f_im = ((abar_im * a_re - (abar_re - 1.0) * a_im) * inv)[..., None]
    b_re, b_im = b_re.astype(F32), b_im.astype(F32)
    bbar_re = f_re * b_re - f_im * b_im
    bbar_im = f_re * b_im + f_im * b_re
    eye = jnp.eye(SSM_GROUPS, dtype=F32)
    half = N_STATE // 2

    def diag_in(m):
        t = jnp.einsum("gpc,gh->gchp", m, eye).reshape(W_C, N_STATE)
        return jnp.stack([t[0:256, 0:half], t[256:512, half:N_STATE]]).astype(BF16)

    def diag_out(m):
        t = jnp.einsum("gcp,gh->gphc", m, eye).reshape(N_STATE, W_C)
        return jnp.stack([t[0:half, 0:256], t[half:N_STATE, 256:512]]).astype(BF16)

    return dict(
        a_re=abar_re.reshape(1, N_STATE), a_im=abar_im.reshape(1, N_STATE),
        b_re=diag_in(bbar_re), b_im=diag_in(bbar_im),
        c_re=diag_out(c_re.astype(F32)), c_imn=diag_out(-c_im.astype(F32)),
    )


def _pool_weight_pairs(pool_w):
    z = jnp.zeros((POOL_GROUP, POOL_GROUP), pool_w.dtype)
    pair = lambda a, b: jnp.concatenate([jnp.concatenate([a, z], axis=1), jnp.concatenate([z, b], axis=1)], axis=0)
    return jnp.stack([pair(pool_w[0], pool_w[1]), pair(pool_w[2], pool_w[3])]).astype(BF16)


def _sink_rows_prompt(sinks):
    return jnp.repeat(sinks.astype(F32).reshape(N_HEADS // 2, 2), LANES, axis=1)


def _sink_rows_sample(sinks):
    pairs = _sink_rows_prompt(sinks).reshape(KV_HEADS, 1, 2, 2 * LANES)
    rows = jnp.broadcast_to(pairs, (KV_HEADS, DEC_SEQ, 2, 2 * LANES)).reshape(16, 2 * LANES)
    return jnp.tile(rows, (ATT_SEQS, 1))


def kernel(x_prompt, x_sample, cache_win_k, cache_win_v, state_pool, state_ssm_re, state_ssm_im, c_prompt, c_sample, norm1_g, norm2_g, w_ada, b_ada, w_in, pool_w, pool_scale, attn_sinks, ssm_a_re, ssm_a_im, ssm_log_dt, ssm_b_re, ssm_b_im, ssm_c_re, ssm_c_im, ssm_d, w_glu, w_branch_a, w_branch_b, w_branch_c, w_out, w_ffn_in, w_ffn_out, final_norm_g):
    mod_all = _ada_call(jnp.concatenate([c_prompt, c_sample], axis=0), w_ada, b_ada)
    tabs_p = _rope_tables(jnp.arange(SEQ, dtype=jnp.int32))
    tabs_s = _rope_tables(PAST_LEN + jnp.arange(DEC_SEQ, dtype=jnp.int32))
    g_final = final_norm_g.reshape(1, D_MODEL)
    zeros_state = jnp.zeros((BATCH, N_STATE), F32)

    hp = x_prompt
    hs = x_sample.reshape(DEC_BATCH, DEC_SEQ * D_MODEL)
    outs = [[] for _ in range(10)]
    for l in range(DEPTH):
        mod_p = mod_all[l, :BATCH].reshape(BATCH, 1, 6 * D_MODEL)
        mod_s = mod_all[l, BATCH:]
        g1 = norm1_g[l].reshape(1, D_MODEL)
        g2 = norm2_g[l].reshape(1, D_MODEL)
        w_mix = w_in[l, :, :MIX_COLS].astype(BF16)
        wts = dict(w_gate=w_in[l, :, MIX_COLS:].astype(BF16), w_a=w_branch_a[l].astype(BF16),
                   w_b=w_branch_b[l].astype(BF16), w_c=w_branch_c[l].astype(BF16), w_out=w_out[l].astype(BF16))
        pw = _pool_weight_pairs(pool_w[l])
        ps = pool_scale[l].reshape(1, W_A)
        sp = _ssm_params(ssm_a_re[l], ssm_a_im[l], ssm_log_dt[l], ssm_b_re[l], ssm_b_im[l], ssm_c_re[l], ssm_c_im[l])
        d_skip = ssm_d[l].reshape(1, W_C)
        wglu = w_glu[l].astype(BF16)
        wfi = w_ffn_in[l].astype(BF16)
        wfo = w_ffn_out[l].astype(BF16)
        last = l == DEPTH - 1

        ya, q, kv, u, kwin, vwin, pool_new = _proj_prompt_call(hp, mod_p, g1, w_mix, pw, ps, tabs_p)
        yb = _attn_prompt_call(q, kv, attn_sinks[l].astype(F32))
        yc, hre, him = _ssm_call(u.reshape(BATCH, SEQ, W_C), zeros_state, zeros_state, sp, d_skip, wglu,
                                 BATCH, SSM_TC, True, "ssm_prompt")
        hp = _merge_call(False, hp, mod_p, g1, ya, yb, yc.reshape(BATCH * SEQ, W_C), wts)
        hp = _ffn_call(False, last, hp, mod_p, g2, wfi, wfo, g_final)
        for k, v in zip(range(5), (kwin.reshape(BATCH, WINDOW, KV_HEADS, HEAD_DIM),
                                   vwin.reshape(BATCH, WINDOW, KV_HEADS, HEAD_DIM), pool_new,
                                   hre.reshape(BATCH, SSM_GROUPS, SSM_STATE),
                                   him.reshape(BATCH, SSM_GROUPS, SSM_STATE))):
            outs[k].append(v)

        ya, q, kn, vn, u, pool_new = _proj_sample_call(
            hs, mod_s, g1, w_mix, pw, ps, tabs_s, state_pool[l].reshape(DEC_BATCH, POOL_BUF * W_A))
        yb, k_new, v_new = _attn_sample_call(
            q, cache_win_k[l].reshape(DEC_BATCH, WINDOW, KV_W), cache_win_v[l].reshape(DEC_BATCH, WINDOW, KV_W),
            kn, vn, _sink_rows_sample(attn_sinks[l]))
        yc, hre, him = _ssm_call(u, state_ssm_re[l].reshape(DEC_BATCH, N_STATE),
                                 state_ssm_im[l].reshape(DEC_BATCH, N_STATE), sp, d_skip, wglu,
                                 DEC_BATCH, DEC_SEQ, False, "ssm_sample")
        hs = _merge_call(True, hs, mod_s, g1, ya, yb, yc, wts)
        hs = _ffn_call(True, last, hs, mod_s, g2, wfi, wfo, g_final)
        for k, v in zip(range(5, 10), (k_new.reshape(DEC_BATCH, WINDOW, KV_HEADS, HEAD_DIM),
                                       v_new.reshape(DEC_BATCH, WINDOW, KV_HEADS, HEAD_DIM),
                                       pool_new.reshape(DEC_BATCH, POOL_BUF, W_A),
                                       hre.reshape(DEC_BATCH, SSM_GROUPS, SSM_STATE),
                                       him.reshape(DEC_BATCH, SSM_GROUPS, SSM_STATE))):
            outs[k].append(v)

    return (hp, hs.reshape(DEC_BATCH, DEC_SEQ, D_MODEL), *[jnp.stack(o) for o in outs])
```

```python
import functools

import jax
import jax.numpy as jnp
from jax import lax
from jax.experimental import pallas as pl
from jax.experimental.pallas import tpu as pltpu

F32 = jnp.float32
BF16 = jnp.bfloat16

D_MODEL = 1024
BATCH = 8
SEQ = 2048
DEPTH = 4
DEC_BATCH = 128
DEC_SEQ = 4
PAST_LEN = 8192

W_A = 512
POOL_WINDOWS = (2, 4, 8, 16)
POOL_GROUP = 128
POOL_BUF = 15
POOL_HIST = 32
HEAD_DIM = 64
N_HEADS = 8
KV_HEADS = 2
W_B = 512
KV_W = 128
WINDOW = 128
ROT_DIM = 16
ROPE_THETA = 500000.0
W_C = 512
SSM_CH = 16
SSM_GROUPS = 32
SSM_STATE = 64
N_STATE = SSM_GROUPS * SSM_STATE
D_FF = 2816
EPS = 1e-6
MIX_COLS = W_A + W_B + 2 * KV_W + W_C

LANES = 128
SUBLANES = 8
VMEM_LIMIT = 56 * 1024 * 1024

TM = 512
NT = SEQ // TM
PROJ_SUB = 128
ATT_BLK = WINDOW
ATT_SUB = 2
ATT_SEQS = 8
SSM_TC = 64
SSM_SUB = 128
SCAN_CW = 512
NEG = -1e30
FF_CHUNKS = ((0, 1024), (1024, 1024), (2048, 768))
PROMPT_MOD_BLOCK = DEC_BATCH // BATCH


def _cparams(sem):
    return pltpu.CompilerParams(dimension_semantics=sem, vmem_limit_bytes=VMEM_LIMIT)


def _dot(a, b):
    return jnp.dot(a, b, preferred_element_type=F32)


def _norm_mod(x, g, sc, sh):
    r = lax.rsqrt(jnp.mean(x * x, axis=-1, keepdims=True) + EPS)
    return (x * r) * g * (1.0 + sc) + sh


def _mod_rows(ref, sample):
    return ref[...] if sample else ref[pl.ds(pl.program_id(0), 1), :]


def _rope(x, cos, sa, sb):
    return x * cos + pltpu.roll(x, 8, axis=1) * sa + pltpu.roll(x, LANES - 8, axis=1) * sb


def _layer_spec(l, tail, **kw):
    nz = (0,) * len(tail)
    return pl.BlockSpec((None, *tail), lambda *_: (l, *nz), **kw)


def _mod_spec(l, sample, j):
    if sample:
        return pl.BlockSpec((None, DEC_BATCH, D_MODEL), lambda *_: (l, 0, j))
    return pl.BlockSpec((None, BATCH, D_MODEL), lambda *_: (l, PROMPT_MOD_BLOCK, j))


def _ada_kernel(c_ref, w_ref, b_ref, o_ref):
    c = c_ref[...]
    a = (c * jax.nn.sigmoid(c)).astype(BF16)
    o_ref[...] = _dot(a, w_ref[...].astype(BF16)) + b_ref[...]


def _ada_call(c_all, w_ada, b_ada):
    n = c_all.shape[0]
    cb = 1536
    return pl.pallas_call(
        _ada_kernel,
        grid=(DEPTH, 6 * D_MODEL // cb),
        in_specs=[
            pl.BlockSpec((n, D_MODEL), lambda l, j: (0, 0)),
            pl.BlockSpec((None, D_MODEL, cb), lambda l, j: (l, 0, j)),
            pl.BlockSpec((None, 1, cb), lambda l, j: (l, 0, j)),
        ],
        out_specs=pl.BlockSpec((None, n, cb), lambda l, j: (l, 0, j)),
        out_shape=jax.ShapeDtypeStruct((DEPTH, n, 6 * D_MODEL), F32),
        compiler_params=_cparams(("arbitrary", "arbitrary")),
        name="ada",
    )(c_all, w_ada, b_ada.reshape(DEPTH, 1, 6 * D_MODEL))


def _proj_prompt_kernel(x_ref, sh_ref, sc_ref, g_ref, w_ref, pw_ref, ps_ref, cos_ref, sa_ref, sb_ref,
                        ya_ref, q_ref, kv_ref, u_ref, kwin_ref, vwin_ref, pool_ref, xe_ref, s2_ref, s4_ref, s8_ref):
    t = pl.program_id(1)
    tm = x_ref.shape[0]
    hist = POOL_HIST
    last = t == pl.num_programs(1) - 1
    sh = _mod_rows(sh_ref, False)
    sc = _mod_rows(sc_ref, False)
    g1, g2 = POOL_GROUP, 2 * POOL_GROUP

    @pl.when(t == 0)
    def _():
        xe_ref[0:hist, :] = jnp.zeros((hist, W_A), F32)

    for r0 in range(0, tm, PROJ_SUB):
        rows = slice(r0, r0 + PROJ_SUB)
        h = _norm_mod(x_ref[rows, :], g_ref[...], sc, sh).astype(BF16)
        z = _dot(h, w_ref[...])
        xa = z[:, 0:W_A]

        a = hist + r0
        e = a + PROJ_SUB
        xe_ref[a:e, :] = xa
        s2_ref[a - 24:e, :] = xe_ref[a - 24:e, :] + xe_ref[a - 25:e - 1, :]
        s4_ref[a - 16:e, :] = s2_ref[a - 16:e, g1:] + s2_ref[a - 18:e - 2, g1:]
        s8_ref[a - 8:e, :] = s4_ref[a - 8:e, g1:] + s4_ref[a - 12:e - 4, g1:]
        sums = (s2_ref[a:e, 0:g1], s4_ref[a:e, 0:g1], s8_ref[a:e, 0:g1],
                s8_ref[a:e, g1:g2] + s8_ref[a - 8:e - 8, g1:g2])
        pos = t * tm + r0 + lax.broadcasted_iota(jnp.int32, (PROJ_SUB, 1), 0)
        ds = []
        for g, w in enumerate(POOL_WINDOWS):
            cnt = jnp.minimum(pos + 1, w).astype(F32)
            ds.append((sums[g] / cnt - xa[:, g * POOL_GROUP:(g + 1) * POOL_GROUP]).astype(BF16))
        for k in range(2):
            y = _dot(jnp.concatenate(ds[2 * k:2 * k + 2], axis=1), pw_ref[k]) * ps_ref[:, k * g2:(k + 1) * g2]
            ya_ref[rows, k * g2:(k + 1) * g2] = y.astype(BF16)

        cos = cos_ref[rows, :]
        sa = sa_ref[rows, :]
        sb = sb_ref[rows, :]
        for c in range(W_B // LANES):
            qc = z[:, W_A + c * LANES:W_A + (c + 1) * LANES]
            q_ref[rows, c * LANES:(c + 1) * LANES] = (_rope(qc, cos, sa, sb) * (HEAD_DIM ** -0.5)).astype(BF16)
        kz = _rope(z[:, W_A + W_B:W_A + W_B + KV_W], cos, sa, sb)
        vz = z[:, W_A + W_B + KV_W:W_A + W_B + 2 * KV_W]
        kv_ref[rows, 0:128] = kz.astype(BF16)
        kv_ref[rows, 128:256] = pltpu.roll(kz, HEAD_DIM, axis=1).astype(BF16)
        kv_ref[rows, 256:384] = vz.astype(BF16)
        kv_ref[rows, 384:512] = pltpu.roll(vz, HEAD_DIM, axis=1).astype(BF16)
        if r0 + PROJ_SUB == tm:
            @pl.when(last)
            def _(kz=kz, vz=vz):
                kwin_ref[...] = kz[PROJ_SUB - WINDOW:, :]
                vwin_ref[...] = vz[PROJ_SUB - WINDOW:, :]

        u_ref[rows, :] = z[:, W_A + W_B + 2 * KV_W:MIX_COLS]

    @pl.when(last)
    def _():
        pool_ref[...] = xe_ref[hist + tm - POOL_BUF:hist + tm, :]

    xe_ref[0:hist, :] = xe_ref[tm:tm + hist, :]


def _proj_prompt_call(l, x, mod, wts, rope_tabs):
    rows = BATCH * SEQ
    row_blk = lambda b, t: (b * NT + t, 0)
    tab_spec = pl.BlockSpec((TM, LANES), lambda b, t: (t, 0))
    return pl.pallas_call(
        _proj_prompt_kernel,
        grid=(BATCH, NT),
        in_specs=[
            pl.BlockSpec((None, TM, D_MODEL), lambda b, t: (b, t, 0)),
            _mod_spec(l, False, 0), _mod_spec(l, False, 1),
            _layer_spec(l, (1, D_MODEL)),
            _layer_spec(l, (D_MODEL, MIX_COLS)),
            _layer_spec(l, (2, 2 * POOL_GROUP, 2 * POOL_GROUP)),
            _layer_spec(l, (1, W_A)),
            tab_spec, tab_spec, tab_spec,
        ],
        out_specs=[
            pl.BlockSpec((TM, W_A), row_blk),
            pl.BlockSpec((TM, W_B), row_blk),
            pl.BlockSpec((TM, 4 * KV_W), row_blk),
            pl.BlockSpec((TM, W_C), row_blk),
            pl.BlockSpec((None, WINDOW, KV_W), lambda b, t: (b, 0, 0)),
            pl.BlockSpec((None, WINDOW, KV_W), lambda b, t: (b, 0, 0)),
            pl.BlockSpec((None, POOL_BUF, W_A), lambda b, t: (b, 0, 0)),
        ],
        out_shape=[
            jax.ShapeDtypeStruct((rows, W_A), BF16),
            jax.ShapeDtypeStruct((rows, W_B), BF16),
            jax.ShapeDtypeStruct((rows, 4 * KV_W), BF16),
            jax.ShapeDtypeStruct((rows, W_C), F32),
            jax.ShapeDtypeStruct((BATCH, WINDOW, KV_W), F32),
            jax.ShapeDtypeStruct((BATCH, WINDOW, KV_W), F32),
            jax.ShapeDtypeStruct((BATCH, POOL_BUF, W_A), F32),
        ],
        scratch_shapes=[pltpu.VMEM((TM + POOL_HIST, W_A), F32),
                        pltpu.VMEM((TM + POOL_HIST, W_A), F32),
                        pltpu.VMEM((TM + POOL_HIST, W_A - POOL_GROUP), F32),
                        pltpu.VMEM((TM + POOL_HIST, W_A - 2 * POOL_GROUP), F32)],
        compiler_params=_cparams(("arbitrary", "arbitrary")),
        name="proj_prompt",
    )(x, mod, mod, wts["g1"], wts["w_mix"], wts["pool_w"], wts["pool_scale"], *rope_tabs)


def _head_pair_operand(lo_mask, a, b):
    zero = jnp.zeros_like(a)
    return jnp.concatenate([jnp.where(lo_mask, a, zero), jnp.where(lo_mask, zero, b)], axis=0)


def _attn_window_block(l, q_ref, yb_ref, sink_ref, r0, kv_prev, kv_cur, prev_bias):
    n = ATT_BLK
    lo = lax.broadcasted_iota(jnp.int32, (n, LANES), 1) < HEAD_DIM
    rows = jnp.bitwise_and(lax.broadcasted_iota(jnp.int32, (2 * n, LANES), 0), n - 1)
    from_prev = lax.broadcasted_iota(jnp.int32, (2 * n, LANES), 1) > rows
    lo2 = lax.broadcasted_iota(jnp.int32, (2 * n, LANES), 1) < HEAD_DIM
    top = lax.broadcasted_iota(jnp.int32, (2 * n, 1), 0) < n
    zero = jnp.zeros((n, LANES), BF16)
    ones_a = jnp.where(lo, 1.0, 0.0).astype(BF16)
    ones_b = jnp.where(lo, 0.0, 1.0).astype(BF16)
    nt_dims = (((1,), (1,)), ((), ()))
    for g in range(KV_HEADS):
        ca, cb = (0, LANES) if g == 0 else (LANES, 0)
        kk = jnp.concatenate([jnp.where(lo, kv_prev[:, ca:ca + LANES], zero),
                              jnp.where(lo, kv_cur[:, ca:ca + LANES], zero),
                              jnp.where(lo, zero, kv_prev[:, cb:cb + LANES]),
                              jnp.where(lo, zero, kv_cur[:, cb:cb + LANES])], axis=0)
        va, vb = 2 * LANES + ca, 2 * LANES + cb
        vv = jnp.concatenate([
            jnp.concatenate([jnp.where(lo, kv_prev[:, va:va + LANES], zero), ones_a], axis=1),
            jnp.concatenate([jnp.where(lo, kv_cur[:, va:va + LANES], zero), ones_a], axis=1),
            jnp.concatenate([jnp.where(lo, zero, kv_prev[:, vb:vb + LANES]), ones_b], axis=1),
            jnp.concatenate([jnp.where(lo, zero, kv_cur[:, vb:vb + LANES]), ones_b], axis=1)], axis=0)
        qs = jnp.concatenate([q_ref[pl.ds(r0, n), (2 * g) * LANES:(2 * g + 1) * LANES],
                              q_ref[pl.ds(r0, n), (2 * g + 1) * LANES:(2 * g + 2) * LANES]], axis=0)
        s = lax.dot_general(qs, kk, nt_dims, preferred_element_type=F32)
        ps, es = [], []
        for hh in range(2):
            s_prev = s[:, hh * 2 * n:hh * 2 * n + n]
            if prev_bias is not None:
                s_prev = s_prev + prev_bias
            logits = jnp.where(from_prev, s_prev, s[:, hh * 2 * n + n:(hh + 1) * 2 * n])
            sk = jnp.where(top, sink_ref[l, 4 * g + hh], sink_ref[l, 4 * g + 2 + hh])
            m = jnp.maximum(jnp.max(logits, axis=-1, keepdims=True), sk)
            p = jnp.exp(logits - m)
            ps += [jnp.where(from_prev, p, 0.0), jnp.where(from_prev, 0.0, p)]
            es.append(jnp.exp(sk - m))
        r = _dot(jnp.concatenate(ps, axis=1).astype(BF16), vv)
        o = r[:, 0:LANES] / (r[:, LANES:2 * LANES] + jnp.where(lo2, es[0], es[1]))
        yb_ref[pl.ds(r0, n), (2 * g) * LANES:(2 * g + 1) * LANES] = o[0:n].astype(BF16)
        yb_ref[pl.ds(r0, n), (2 * g + 1) * LANES:(2 * g + 2) * LANES] = o[n:2 * n].astype(BF16)


def _attn_prompt_kernel(l, sink_ref, q_ref, kvc_ref, kvp_ref, yb_ref):
    i = pl.program_id(1)
    first_bias = jnp.where(i > 0, 0.0, NEG).astype(F32)
    for sub in range(ATT_SUB):
        r0 = sub * ATT_BLK
        kv_cur = kvc_ref[r0:r0 + ATT_BLK, :]
        kv_prev = kvp_ref[...] if sub == 0 else kvc_ref[r0 - ATT_BLK:r0, :]
        _attn_window_block(l, q_ref, yb_ref, sink_ref, r0, kv_prev, kv_cur, first_bias if sub == 0 else None)


def _attn_prompt_call(l, q, kv, sinks):
    rows = ATT_SUB * ATT_BLK
    nb = SEQ // rows
    return pl.pallas_call(
        functools.partial(_attn_prompt_kernel, l),
        grid=(BATCH, nb),
        in_specs=[
            pl.BlockSpec(memory_space=pltpu.SMEM),
            pl.BlockSpec((rows, W_B), lambda b, i: (b * nb + i, 0)),
            pl.BlockSpec((rows, 4 * KV_W), lambda b, i: (b * nb + i, 0)),
            pl.BlockSpec((ATT_BLK, 4 * KV_W), lambda b, i: ((b * nb + i) * ATT_SUB - jnp.minimum(i, 1), 0)),
        ],
        out_specs=pl.BlockSpec((rows, W_B), lambda b, i: (b * nb + i, 0)),
        out_shape=jax.ShapeDtypeStruct((BATCH * SEQ, W_B), BF16),
        compiler_params=_cparams(("arbitrary", "arbitrary")),
        name="attn_prompt",
    )(sinks, q, kv, kv)


def _ssm_kernel(rows_per_step, seq_major, u_ref, h0re_ref, h0im_ref, are_ref, aim_ref, bre_ref, bim_ref, cre_ref,
                cimn_ref, d_ref, wglu_ref, yc_ref, hre_out, him_out, dre, dim_, hre, him, *slabs):
    i = pl.program_id(0)
    r = rows_per_step
    half = N_STATE // 2
    nslab = W_C // LANES
    blk = dre.shape[0]
    steps = SSM_SUB // r

    @pl.when(i == 0)
    def _():
        hre[...] = h0re_ref[...]
        him[...] = h0im_ref[...]

    if seq_major:
        (perm,) = slabs
        tc = u_ref.shape[1]
        for b in range(r):
            for c in range(nslab):
                perm[c, pl.ds(b, tc, stride=r), :] = u_ref[b, :, c * LANES:(c + 1) * LANES]

    def u_rows(rows):
        if seq_major:
            return jnp.concatenate([perm[c, rows, :] for c in range(nslab)], axis=1)
        return u_ref[rows, :]

    def drive(s):
        rows = slice(s * SSM_SUB, (s + 1) * SSM_SUB)
        ub = u_rows(rows).astype(BF16)
        for kt in range(2):
            uk = ub[:, kt * 256:(kt + 1) * 256]
            dre[rows, kt * half:(kt + 1) * half] = _dot(uk, bre_ref[kt])
            dim_[rows, kt * half:(kt + 1) * half] = _dot(uk, bim_ref[kt])

    def scan(s):
        for rt in range(r // SUBLANES):
            rsl = slice(rt * SUBLANES, (rt + 1) * SUBLANES)
            for c in range(N_STATE // SCAN_CW):
                cols = slice(c * SCAN_CW, (c + 1) * SCAN_CW)
                ar = jnp.broadcast_to(are_ref[:, cols], (SUBLANES, SCAN_CW))
                ai = jnp.broadcast_to(aim_ref[:, cols], (SUBLANES, SCAN_CW))
                pr, pi = hre[rsl, cols], him[rsl, cols]
                for t in range(steps):
                    row = s * SSM_SUB + t * r + rt * SUBLANES
                    trow = slice(row, row + SUBLANES)
                    nr = ar * pr - ai * pi + dre[trow, cols]
                    ni = ar * pi + ai * pr + dim_[trow, cols]
                    dre[trow, cols] = nr
                    dim_[trow, cols] = ni
                    pr, pi = nr, ni
                hre[rsl, cols] = pr
                him[rsl, cols] = pi

    def project(s):
        rows = slice(s * SSM_SUB, (s + 1) * SSM_SUB)
        parts = []
        for nt in range(2):
            sl = slice(nt * half, (nt + 1) * half)
            parts.append(_dot(dre[rows, sl].astype(BF16), cre_ref[nt]) + _dot(dim_[rows, sl].astype(BF16), cimn_ref[nt]))
        return jnp.concatenate(parts, axis=1)

    nsub = blk // SSM_SUB
    drive(0)
    ys = []
    for s in range(nsub):
        if s + 1 < nsub:
            drive(s + 1)
        scan(s)
        ys.append(project(s))

    @pl.when(i == pl.num_programs(0) - 1)
    def _():
        hre_out[...] = hre[...]
        him_out[...] = him[...]

    y = jnp.concatenate(ys, axis=0)
    if seq_major:
        for c in range(nslab):
            perm[c] = y[:, c * LANES:(c + 1) * LANES]
        y = jnp.concatenate(
            [jnp.concatenate([perm[c, pl.ds(b, tc, stride=r), :] for c in range(nslab)], axis=1) for b in range(r)],
            axis=0)
        u = u_ref[...].reshape(r * tc, W_C)
    else:
        u = u_ref[...]
    y = jax.nn.gelu(y + d_ref[...] * u)
    yc = (y * jax.nn.sigmoid(_dot(y.astype(BF16), wglu_ref[...]))).astype(BF16)
    yc_ref[...] = yc.reshape(yc_ref.shape)


def _ssm_call(l, u, h0re, h0im, h0_layer, wts, rows_per_step, steps_per_block, seq_major, name):
    blk = rows_per_step * steps_per_block
    half = N_STATE // 2
    if seq_major:
        nsteps = u.shape[1] // steps_per_block
        io_spec = pl.BlockSpec((rows_per_step, steps_per_block, W_C), lambda i: (0, i, 0))
        scratch = [pltpu.VMEM((W_C // LANES, blk, LANES), F32)]
    else:
        nsteps = u.shape[0] // blk
        io_spec = pl.BlockSpec((blk, W_C), lambda i: (i, 0))
        scratch = []
    state_spec = pl.BlockSpec((rows_per_step, N_STATE), lambda i: (0, 0))
    h0_spec = _layer_spec(h0_layer, (rows_per_step, N_STATE))
    return pl.pallas_call(
        functools.partial(_ssm_kernel, rows_per_step, seq_major),
        grid=(nsteps,),
        in_specs=[
            io_spec, h0_spec, h0_spec,
            _layer_spec(l, (1, N_STATE)), _layer_spec(l, (1, N_STATE)),
            _layer_spec(l, (2, 256, half)), _layer_spec(l, (2, 256, half)),
            _layer_spec(l, (2, half, 256)), _layer_spec(l, (2, half, 256)),
            _layer_spec(l, (1, W_C)),
            _layer_spec(l, (W_C, W_C)),
        ],
        out_specs=[io_spec, state_spec, state_spec],
        out_shape=[
            jax.ShapeDtypeStruct(u.shape, BF16),
            jax.ShapeDtypeStruct((rows_per_step, N_STATE), F32),
            jax.ShapeDtypeStruct((rows_per_step, N_STATE), F32),
        ],
        scratch_shapes=[
            pltpu.VMEM((blk, N_STATE), F32),
            pltpu.VMEM((blk, N_STATE), F32),
            pltpu.VMEM((rows_per_step, N_STATE), F32),
            pltpu.VMEM((rows_per_step, N_STATE), F32),
        ] + scratch,
        compiler_params=_cparams(("arbitrary",)),
        name=name,
    )(u, h0re, h0im, wts["ssm_a_re"], wts["ssm_a_im"], wts["ssm_b_re"], wts["ssm_b_im"], wts["ssm_c_re"],
      wts["ssm_c_imn"], wts["ssm_d"], wts["w_glu"])


def _merge_kernel(sample, x_ref, sh_ref, sc_ref, gt_ref, g_ref, ya_ref, yb_ref, yc_ref,
                  wg_ref, wa_ref, wb_ref, wc_ref, wo_ref, o_ref):
    x = x_ref[...]
    h = _norm_mod(x, g_ref[...], _mod_rows(sc_ref, sample), _mod_rows(sh_ref, sample)).astype(BF16)
    if sample:
        t = pl.program_id(1)
        parts = []
        for j in range(N_HEADS // 2):
            start = (j // 2) * 8 + t * 2 + (j % 2)
            parts.append(yb_ref[pl.ds(start, DEC_BATCH, stride=16), :])
        yb = jnp.concatenate(parts, axis=1).astype(BF16)
    else:
        yb = yb_ref[...]
    merged = None
    for k, (y, w_ref) in enumerate(((ya_ref[...], wa_ref), (yb, wb_ref), (yc_ref[...], wc_ref))):
        gate = jax.nn.sigmoid(_dot(h, wg_ref[:, k * D_MODEL:(k + 1) * D_MODEL]))
        term = gate * _dot(y, w_ref[...])
        merged = term if merged is None else merged + term
    o_ref[...] = x + _mod_rows(gt_ref, sample) * _dot(merged.astype(BF16), wo_ref[...])


def _row_specs(sample):
    if sample:
        tm = DEC_BATCH
        grid = (1, DEC_SEQ)
        x_spec = pl.BlockSpec((tm, D_MODEL), lambda b, t: (0, t))
        row_spec = lambda cols: pl.BlockSpec((tm, cols), lambda b, t: (t, 0))
    else:
        tm = TM
        grid = (BATCH, NT)
        x_spec = pl.BlockSpec((None, tm, D_MODEL), lambda b, t: (b, t, 0))
        row_spec = lambda cols: pl.BlockSpec((tm, cols), lambda b, t: (b * NT + t, 0))
    return grid, x_spec, row_spec


def _merge_call(l, sample, x, mod, ya, yb, yc, wts):
    grid, x_spec, row_spec = _row_specs(sample)
    yb_spec = pl.BlockSpec(yb.shape, lambda b, t: (0, 0)) if sample else row_spec(W_B)
    return pl.pallas_call(
        functools.partial(_merge_kernel, sample),
        grid=grid,
        in_specs=[
            x_spec, _mod_spec(l, sample, 0), _mod_spec(l, sample, 1), _mod_spec(l, sample, 2),
            _layer_spec(l, (1, D_MODEL)),
            row_spec(W_A), yb_spec, row_spec(W_C),
            _layer_spec(l, (D_MODEL, 3 * D_MODEL)),
            _layer_spec(l, (W_A, D_MODEL)), _layer_spec(l, (W_B, D_MODEL)), _layer_spec(l, (W_C, D_MODEL)),
            _layer_spec(l, (D_MODEL, D_MODEL)),
        ],
        out_specs=x_spec,
        out_shape=jax.ShapeDtypeStruct(x.shape, F32),
        compiler_params=_cparams(("arbitrary", "arbitrary")),
        name="merge_sample" if sample else "merge_prompt",
    )(x, mod, mod, mod, wts["g1"], ya, yb, yc, wts["w_gate"], wts["w_a"], wts["w_b"], wts["w_c"], wts["w_out"])


def _ffn_kernel(sample, final, x_ref, sh_ref, sc_ref, gt_ref, g_ref, wi_ref, wo_ref, gf_ref, o_ref):
    x = x_ref[...]
    h = _norm_mod(x, g_ref[...], _mod_rows(sc_ref, sample), _mod_rows(sh_ref, sample)).astype(BF16)
    acc = None
    for lo, n in FF_CHUNKS:
        a = _dot(h, wi_ref[:, lo:lo + n])
        b = _dot(h, wi_ref[:, D_FF + lo:D_FF + lo + n])
        act = ((a * jax.nn.sigmoid(a)) * b).astype(BF16)
        part = _dot(act, wo_ref[lo:lo + n, :])
        acc = part if acc is None else acc + part
    y = x + _mod_rows(gt_ref, sample) * acc
    if final:
        r = lax.rsqrt(jnp.mean(y * y, axis=-1, keepdims=True) + EPS)
        y = (y * r) * gf_ref[...]
    o_ref[...] = y


def _ffn_call(l, sample, final, x, mod, wts, g_final):
    grid, x_spec, _ = _row_specs(sample)
    single = pl.Buffered(1)
    return pl.pallas_call(
        functools.partial(_ffn_kernel, sample, final),
        grid=grid,
        in_specs=[
            x_spec, _mod_spec(l, sample, 3), _mod_spec(l, sample, 4), _mod_spec(l, sample, 5),
            _layer_spec(l, (1, D_MODEL)),
            _layer_spec(l, (D_MODEL, 2 * D_FF), pipeline_mode=single),
            _layer_spec(l, (D_FF, D_MODEL), pipeline_mode=single),
            pl.BlockSpec((1, D_MODEL), lambda b, t: (0, 0)),
        ],
        out_specs=x_spec,
        out_shape=jax.ShapeDtypeStruct(x.shape, F32),
        compiler_params=_cparams(("arbitrary", "arbitrary")),
        name="ffn_sample" if sample else "ffn_prompt",
    )(x, mod, mod, mod, wts["g2"], wts["w_ffn_in"], wts["w_ffn_out"], g_final)


def _proj_sample_kernel(x_ref, sh_ref, sc_ref, g_ref, w_ref, pw_ref, ps_ref, cos_ref, sa_ref, sb_ref, pool_prev_ref,
                        ya_ref, q_ref, kn_ref, vn_ref, u_ref, pool_ref):
    nb = DEC_BATCH
    sh = sh_ref[...]
    sc = sc_ref[...]
    h = jnp.concatenate(
        [_norm_mod(x_ref[:, t * D_MODEL:(t + 1) * D_MODEL], g_ref[...], sc, sh) for t in range(DEC_SEQ)],
        axis=0).astype(BF16)

    z = _dot(h, w_ref[...])
    xa = z[:, 0:W_A]
    xe = [pool_prev_ref[:, j * W_A:(j + 1) * W_A] for j in range(POOL_BUF)]
    xe += [xa[t * nb:(t + 1) * nb, :] for t in range(DEC_SEQ)]
    for j in range(POOL_BUF):
        pool_ref[:, j * W_A:(j + 1) * W_A] = xe[DEC_SEQ + j]
    dgs = []
    for g, w in enumerate(POOL_WINDOWS):
        lo = g * POOL_GROUP
        ds = []
        for t in range(DEC_SEQ):
            s = xe[POOL_BUF + t][:, lo:lo + POOL_GROUP]
            for j in range(1, w):
                s = s + xe[POOL_BUF + t - j][:, lo:lo + POOL_GROUP]
            ds.append(s / float(w) - xe[POOL_BUF + t][:, lo:lo + POOL_GROUP])
        dgs.append(jnp.concatenate(ds, axis=0).astype(BF16))
    for k in range(2):
        cols = slice(2 * k * POOL_GROUP, 2 * (k + 1) * POOL_GROUP)
        y = _dot(jnp.concatenate(dgs[2 * k:2 * k + 2], axis=1), pw_ref[k]) * ps_ref[:, cols]
        ya_ref[:, cols] = y.astype(BF16)

    def tabs(t):
        return cos_ref[t:t + 1, :], sa_ref[t:t + 1, :], sb_ref[t:t + 1, :]

    for j in range(W_B // LANES):
        qc = z[:, W_A + j * LANES:W_A + (j + 1) * LANES]
        for t in range(DEC_SEQ):
            qt = _rope(qc[t * nb:(t + 1) * nb, :], *tabs(t)) * (HEAD_DIM ** -0.5)
            q_ref[pl.ds((j // 2) * 8 + t * 2 + (j % 2), nb, stride=16), :] = qt
    kz = z[:, W_A + W_B:W_A + W_B + KV_W]
    vz = z[:, W_A + W_B + KV_W:W_A + W_B + 2 * KV_W]
    for t in range(DEC_SEQ):
        kn_ref[pl.ds(t, nb, stride=DEC_SEQ), :] = _rope(kz[t * nb:(t + 1) * nb, :], *tabs(t))
        vn_ref[pl.ds(t, nb, stride=DEC_SEQ), :] = vz[t * nb:(t + 1) * nb, :]
    u_ref[...] = z[:, W_A + W_B + 2 * KV_W:MIX_COLS]


def _proj_sample_call(l, x2d, mod, wts, rope_tabs, pool_prev):
    rows = DEC_BATCH * DEC_SEQ
    full = lambda shape: pl.BlockSpec(shape, lambda i: (0,) * len(shape))
    out_shape = [
        jax.ShapeDtypeStruct((rows, W_A), BF16),
        jax.ShapeDtypeStruct((DEC_BATCH * 16, LANES), F32),
        jax.ShapeDtypeStruct((rows, KV_W), F32),
        jax.ShapeDtypeStruct((rows, KV_W), F32),
        jax.ShapeDtypeStruct((rows, W_C), F32),
        jax.ShapeDtypeStruct((DEC_BATCH, POOL_BUF * W_A), F32),
    ]
    return pl.pallas_call(
        _proj_sample_kernel,
        grid=(1,),
        in_specs=[
            full(x2d.shape), _mod_spec(l, True, 0), _mod_spec(l, True, 1),
            _layer_spec(l, (1, D_MODEL)),
            _layer_spec(l, (D_MODEL, MIX_COLS)),
            _layer_spec(l, (2, 2 * POOL_GROUP, 2 * POOL_GROUP)),
            _layer_spec(l, (1, W_A)),
            full(rope_tabs[0].shape), full(rope_tabs[1].shape), full(rope_tabs[2].shape),
            _layer_spec(l, (DEC_BATCH, POOL_BUF * W_A)),
        ],
        out_specs=[full(s.shape) for s in out_shape],
        out_shape=out_shape,
        compiler_params=_cparams(("arbitrary",)),
        name="proj_sample",
    )(x2d, mod, mod, wts["g1"], wts["w_mix"], wts["pool_w"], wts["pool_scale"], *rope_tabs, pool_prev)


def _attn_sample_kernel(q_ref, kc_ref, vc_ref, kn_ref, vn_ref, sink_ref, yb_ref, ko_ref, vo_ref):
    nn = 2 * SUBLANES
    nprob = KV_HEADS * ATT_SEQS
    nrow = nprob * SUBLANES
    lo_c = lax.broadcasted_iota(jnp.int32, (WINDOW, LANES), 1) < HEAD_DIM
    lo_n = lax.broadcasted_iota(jnp.int32, (SUBLANES, LANES), 1) < HEAD_DIM
    pad = jnp.zeros((SUBLANES - DEC_SEQ, LANES), F32)
    nt_dims = (((1,), (1,)), ((), ()))

    for i in range(ATT_SEQS):
        ko_ref[i, 0:WINDOW - DEC_SEQ, :] = kc_ref[i, DEC_SEQ:WINDOW, :]
        ko_ref[i, WINDOW - DEC_SEQ:WINDOW, :] = kn_ref[i * DEC_SEQ:(i + 1) * DEC_SEQ, :]
        vo_ref[i, 0:WINDOW - DEC_SEQ, :] = vc_ref[i, DEC_SEQ:WINDOW, :]
        vo_ref[i, WINDOW - DEC_SEQ:WINDOW, :] = vn_ref[i * DEC_SEQ:(i + 1) * DEC_SEQ, :]

    def pair_operands(cache, new):
        new8 = jnp.concatenate([new, pad], axis=0)
        cr, nr = pltpu.roll(cache, HEAD_DIM, axis=1), pltpu.roll(new8, HEAD_DIM, axis=1)
        out = []
        for g in range(KV_HEADS):
            (ca, cb), (na, nb) = ((cache, cr), (new8, nr)) if g == 0 else ((cr, cache), (nr, new8))
            out.append(jnp.concatenate([_head_pair_operand(lo_c, ca, cb), _head_pair_operand(lo_n, na, nb)],
                                       axis=0).astype(BF16))
        return out

    kks, vvs = [], []
    for i in range(ATT_SEQS):
        kks += pair_operands(kc_ref[i], kn_ref[i * DEC_SEQ:(i + 1) * DEC_SEQ, :])
        vvs += pair_operands(vc_ref[i], vn_ref[i * DEC_SEQ:(i + 1) * DEC_SEQ, :])

    s = jnp.concatenate(
        [lax.dot_general(q_ref[k * SUBLANES:(k + 1) * SUBLANES, :].astype(BF16), kks[k], nt_dims,
                         preferred_element_type=F32) for k in range(nprob)], axis=0)
    tq_c = jnp.right_shift(jnp.bitwise_and(lax.broadcasted_iota(jnp.int32, (nrow, WINDOW), 0), SUBLANES - 1), 1)
    mask_c = lax.broadcasted_iota(jnp.int32, (nrow, WINDOW), 1) > tq_c
    tq_n = jnp.right_shift(jnp.bitwise_and(lax.broadcasted_iota(jnp.int32, (nrow, nn), 0), SUBLANES - 1), 1)
    col_n = lax.broadcasted_iota(jnp.int32, (nrow, nn), 1)
    vis_n = jnp.bitwise_and(col_n, SUBLANES - 1) <= tq_n
    half_n = jnp.right_shift(col_n, 3)
    s_n = s[:, 2 * WINDOW:2 * WINDOW + nn]
    pcs = []
    pn = jnp.zeros((nrow, nn), F32)
    for hh in range(2):
        sk = sink_ref[:, hh * LANES:hh * LANES + 1]
        sc = jnp.where(mask_c, s[:, hh * WINDOW:(hh + 1) * WINDOW], NEG)
        sn = jnp.where(vis_n & (half_n == hh), s_n, NEG)
        m = jnp.maximum(jnp.maximum(jnp.max(sc, axis=-1, keepdims=True), jnp.max(sn, axis=-1, keepdims=True)), sk)
        pc = jnp.exp(sc - m)
        ph = jnp.exp(sn - m)
        den = jnp.sum(pc, axis=-1, keepdims=True) + jnp.sum(ph, axis=-1, keepdims=True) + jnp.exp(sk - m)
        pcs.append(pc / den)
        pn = pn + ph / den
    p = jnp.concatenate(pcs + [pn], axis=1)
    yb_ref[...] = jnp.concatenate(
        [_dot(p[k * SUBLANES:(k + 1) * SUBLANES, :].astype(BF16), vvs[k]) for k in range(nprob)], axis=0)


def _attn_sample_call(l, q, kc, vc, kn, vn, sink_rows):
    bs = ATT_SEQS
    cache_in = pl.BlockSpec((None, bs, WINDOW, KV_W), lambda i: (l, i, 0, 0))
    cache_out = pl.BlockSpec((bs, WINDOW, KV_W), lambda i: (i, 0, 0))
    new_spec = pl.BlockSpec((bs * DEC_SEQ, KV_W), lambda i: (i, 0))
    q_spec = pl.BlockSpec((bs * 16, LANES), lambda i: (i, 0))
    return pl.pallas_call(
        _attn_sample_kernel,
        grid=(DEC_BATCH // bs,),
        in_specs=[q_spec, cache_in, cache_in, new_spec, new_spec, _layer_spec(l, (bs * 16, 2 * LANES))],
        out_specs=[q_spec, cache_out, cache_out],
        out_shape=[
            jax.ShapeDtypeStruct((DEC_BATCH * 16, LANES), F32),
            jax.ShapeDtypeStruct((DEC_BATCH, WINDOW, KV_W), F32),
            jax.ShapeDtypeStruct((DEC_BATCH, WINDOW, KV_W), F32),
        ],
        compiler_params=_cparams(("arbitrary",)),
        name="attn_sample",
    )(q, kc, vc, kn, vn, sink_rows)


def _rope_tables(pos):
    inv = ROPE_THETA ** (-jnp.arange(0, ROT_DIM, 2, dtype=F32) / ROT_DIM)
    ang = pos.astype(F32)[:, None] * inv[None, :]
    cos, sin = jnp.cos(ang), jnp.sin(ang)
    n = pos.shape[0]
    half = ROT_DIM // 2
    rest = HEAD_DIM - ROT_DIM
    z8 = jnp.zeros((n, half), F32)
    zr = jnp.zeros((n, rest), F32)
    c64 = jnp.concatenate([cos, cos, jnp.ones((n, rest), F32)], axis=1)
    sa64 = jnp.concatenate([z8, sin, zr], axis=1)
    sb64 = jnp.concatenate([-sin, z8, zr], axis=1)
    return tuple(jnp.tile(t, (1, LANES // HEAD_DIM)) for t in (c64, sa64, sb64))


def _ssm_params(a_re, a_im, log_dt, b_re, b_im, c_re, c_im):
    a_re, a_im = a_re.astype(F32), a_im.astype(F32)
    dt = jnp.exp(log_dt.astype(F32))[..., None]
    mag = jnp.exp(a_re * dt)
    abar_re, abar_im = mag * jnp.cos(a_im * dt), mag * jnp.sin(a_im * dt)
    inv = 1.0 / (a_re * a_re + a_im * a_im)
    f_re = (((abar_re - 1.0) * a_re + abar_im * a_im) * inv)[..., None]
    f_im = ((abar_im * a_re - (abar_re - 1.0) * a_im) * inv)[..., None]
    b_re, b_im = b_re.astype(F32), b_im.astype(F32)
    bbar_re = f_re * b_re - f_im * b_im
    bbar_im = f_re * b_im + f_im * b_re
    eye = jnp.eye(SSM_GROUPS, dtype=F32)
    half = N_STATE // 2

    def diag_in(m):
        t = jnp.einsum("lgpc,gh->lgchp", m, eye).reshape(DEPTH, W_C, N_STATE)
        return jnp.stack([t[:, 0:256, 0:half], t[:, 256:512, half:N_STATE]], axis=1).astype(BF16)

    def diag_out(m):
        t = jnp.einsum("lgcp,gh->lgphc", m, eye).reshape(DEPTH, N_STATE, W_C)
        return jnp.stack([t[:, 0:half, 0:256], t[:, half:N_STATE, 256:512]], axis=1).astype(BF16)

    return dict(
        ssm_a_re=abar_re.reshape(DEPTH, 1, N_STATE), ssm_a_im=abar_im.reshape(DEPTH, 1, N_STATE),
        ssm_b_re=diag_in(bbar_re), ssm_b_im=diag_in(bbar_im),
        ssm_c_re=diag_out(c_re.astype(F32)), ssm_c_imn=diag_out(-c_im.astype(F32)),
    )


def _pool_weight_pairs(pool_w):
    z = jnp.zeros((DEPTH, POOL_GROUP, POOL_GROUP), pool_w.dtype)
    pair = lambda a, b: jnp.concatenate([jnp.concatenate([a, z], axis=2), jnp.concatenate([z, b], axis=2)], axis=1)
    return jnp.stack([pair(pool_w[:, 0], pool_w[:, 1]), pair(pool_w[:, 2], pool_w[:, 3])], axis=1).astype(BF16)


def _sink_rows_sample(sinks):
    pairs = jnp.repeat(sinks.astype(F32).reshape(DEPTH, N_HEADS // 2, 2), LANES, axis=2)
    pairs = pairs.reshape(DEPTH, 1, KV_HEADS, 1, 2, 2 * LANES)
    rows = jnp.broadcast_to(pairs, (DEPTH, ATT_SEQS, KV_HEADS, DEC_SEQ, 2, 2 * LANES))
    return rows.reshape(DEPTH, ATT_SEQS * 16, 2 * LANES)


def kernel(x_prompt, x_sample, cache_win_k, cache_win_v, state_pool, state_ssm_re, state_ssm_im, c_prompt, c_sample, norm1_g, norm2_g, w_ada, b_ada, w_in, pool_w, pool_scale, attn_sinks, ssm_a_re, ssm_a_im, ssm_log_dt, ssm_b_re, ssm_b_im, ssm_c_re, ssm_c_im, ssm_d, w_glu, w_branch_a, w_branch_b, w_branch_c, w_out, w_ffn_in, w_ffn_out, final_norm_g):
    mod = _ada_call(jnp.concatenate([c_sample, c_prompt], axis=0), w_ada, b_ada)
    tabs_p = _rope_tables(jnp.arange(SEQ, dtype=jnp.int32))
    tabs_s = _rope_tables(PAST_LEN + jnp.arange(DEC_SEQ, dtype=jnp.int32))
    g_final = final_norm_g.reshape(1, D_MODEL)

    wts = dict(
        g1=norm1_g.reshape(DEPTH, 1, D_MODEL), g2=norm2_g.reshape(DEPTH, 1, D_MODEL),
        w_mix=w_in[:, :, :MIX_COLS].astype(BF16), w_gate=w_in[:, :, MIX_COLS:].astype(BF16),
        pool_w=_pool_weight_pairs(pool_w), pool_scale=pool_scale.reshape(DEPTH, 1, W_A),
        ssm_d=ssm_d.reshape(DEPTH, 1, W_C), w_glu=w_glu.astype(BF16),
        w_a=w_branch_a.astype(BF16), w_b=w_branch_b.astype(BF16), w_c=w_branch_c.astype(BF16),
        w_out=w_out.astype(BF16), w_ffn_in=w_ffn_in.astype(BF16), w_ffn_out=w_ffn_out.astype(BF16),
        **_ssm_params(ssm_a_re, ssm_a_im, ssm_log_dt, ssm_b_re, ssm_b_im, ssm_c_re, ssm_c_im),
    )
    sinks_p = attn_sinks.astype(F32)
    sinks_s = _sink_rows_sample(attn_sinks)
    cache_k = cache_win_k.reshape(DEPTH, DEC_BATCH, WINDOW, KV_W)
    cache_v = cache_win_v.reshape(DEPTH, DEC_BATCH, WINDOW, KV_W)
    pool_prev = state_pool.reshape(DEPTH, DEC_BATCH, POOL_BUF * W_A)
    h0_re = state_ssm_re.reshape(DEPTH, DEC_BATCH, N_STATE)
    h0_im = state_ssm_im.reshape(DEPTH, DEC_BATCH, N_STATE)
    zeros_state = jnp.zeros((1, BATCH, N_STATE), F32)

    hp = x_prompt
    hs = x_sample.reshape(DEC_BATCH, DEC_SEQ * D_MODEL)
    outs = [[] for _ in range(10)]
    for l in range(DEPTH):
        last = l == DEPTH - 1

        ya, q, kv, u, kwin, vwin, pool_new = _proj_prompt_call(l, hp, mod, wts, tabs_p)
        yb = _attn_prompt_call(l, q, kv, sinks_p)
        yc, hre, him = _ssm_call(l, u.reshape(BATCH, SEQ, W_C), zeros_state, zeros_state, 0, wts,
                                 BATCH, SSM_TC, True, "ssm_prompt")
        hp = _merge_call(l, False, hp, mod, ya, yb, yc.reshape(BATCH * SEQ, W_C), wts)
        hp = _ffn_call(l, False, last, hp, mod, wts, g_final)
        for k, v in zip(range(5), (kwin.reshape(BATCH, WINDOW, KV_HEADS, HEAD_DIM),
                                   vwin.reshape(BATCH, WINDOW, KV_HEADS, HEAD_DIM), pool_new,
                                   hre.reshape(BATCH, SSM_GROUPS, SSM_STATE),
                                   him.reshape(BATCH, SSM_GROUPS, SSM_STATE))):
            outs[k].append(v)

        ya, q, kn, vn, u, pool_new = _proj_sample_call(l, hs, mod, wts, tabs_s, pool_prev)
        yb, k_new, v_new = _attn_sample_call(l, q, cache_k, cache_v, kn, vn, sinks_s)
        yc, hre, him = _ssm_call(l, u, h0_re, h0_im, l, wts, DEC_BATCH, DEC_SEQ, False, "ssm_sample")
        hs = _merge_call(l, True, hs, mod, ya, yb, yc, wts)
        hs = _ffn_call(l, True, last, hs, mod, wts, g_final)
        for k, v in zip(range(5, 10), (k_new.reshape(DEC_BATCH, WINDOW, KV_HEADS, HEAD_DIM),
                                       v_new.reshape(DEC_BATCH, WINDOW, KV_HEADS, HEAD_DIM),
                                       pool_new.reshape(DEC_BATCH, POOL_BUF, W_A),
                                       hre.reshape(DEC_BATCH, SSM_GROUPS, SSM_STATE),
                                       him.reshape(DEC_BATCH, SSM_GROUPS, SSM_STATE))):
            outs[k].append(v)

    return (hp, hs.reshape(DEC_BATCH, DEC_SEQ, D_MODEL), *[jnp.stack(o) for o in outs])
```

```python
import functools

import jax
import jax.numpy as jnp
from jax import lax
from jax.experimental import pallas as pl
from jax.experimental.pallas import tpu as pltpu

F32 = jnp.float32
BF16 = jnp.bfloat16

D_MODEL = 1024
BATCH = 8
SEQ = 2048
DEPTH = 4
DEC_BATCH = 128
DEC_SEQ = 4
PAST_LEN = 8192

W_A = 512
POOL_WINDOWS = (2, 4, 8, 16)
POOL_GROUP = 128
POOL_BUF = 15
POOL_HIST = 32
HEAD_DIM = 64
N_HEADS = 8
KV_HEADS = 2
W_B = 512
KV_W = 128
WINDOW = 128
ROT_DIM = 16
ROPE_THETA = 500000.0
W_C = 512
SSM_CH = 16
SSM_GROUPS = 32
SSM_STATE = 64
N_STATE = SSM_GROUPS * SSM_STATE
D_FF = 2816
EPS = 1e-6
MIX_COLS = W_A + W_B + 2 * KV_W + W_C

LANES = 128
SUBLANES = 8
VMEM_LIMIT = 56 * 1024 * 1024

TM = 512
NT = SEQ // TM
PROJ_SUB = 128
ATT_BLK = WINDOW
ATT_SUB = 4
TM_WIDE = 1024
ATT_SEQS = 8
SSM_TC = 64
SSM_SUB = 128
SCAN_CW = 512
NEG = -1e30
FF_CHUNKS = ((0, 1024), (1024, 1024), (2048, 768))
PROMPT_MOD_BLOCK = DEC_BATCH // BATCH


def _cparams(sem):
    return pltpu.CompilerParams(dimension_semantics=sem, vmem_limit_bytes=VMEM_LIMIT)


def _dot(a, b):
    return jnp.dot(a, b, preferred_element_type=F32)


def _norm_mod(x, g, sc, sh):
    r = lax.rsqrt(jnp.mean(x * x, axis=-1, keepdims=True) + EPS)
    return (x * r) * g * (1.0 + sc) + sh


def _mod_rows(ref, sample):
    return ref[...] if sample else ref[pl.ds(pl.program_id(0), 1), :]


def _rope(x, cos, sa, sb):
    return x * cos + pltpu.roll(x, 8, axis=1) * sa + pltpu.roll(x, LANES - 8, axis=1) * sb


def _layer_spec(l, tail, **kw):
    nz = (0,) * len(tail)
    return pl.BlockSpec((None, *tail), lambda *_: (l, *nz), **kw)


def _mod_spec(l, sample, j):
    if sample:
        return pl.BlockSpec((None, DEC_BATCH, D_MODEL), lambda *_: (l, 0, j))
    return pl.BlockSpec((None, BATCH, D_MODEL), lambda *_: (l, PROMPT_MOD_BLOCK, j))


def _ada_kernel(c_ref, w_ref, b_ref, o_ref):
    c = c_ref[...]
    a = (c * jax.nn.sigmoid(c)).astype(BF16)
    o_ref[...] = _dot(a, w_ref[...].astype(BF16)) + b_ref[...]


def _ada_call(c_all, w_ada, b_ada):
    n = c_all.shape[0]
    cb = 1536
    return pl.pallas_call(
        _ada_kernel,
        grid=(DEPTH, 6 * D_MODEL // cb),
        in_specs=[
            pl.BlockSpec((n, D_MODEL), lambda l, j: (0, 0)),
            pl.BlockSpec((None, D_MODEL, cb), lambda l, j: (l, 0, j)),
            pl.BlockSpec((None, 1, cb), lambda l, j: (l, 0, j)),
        ],
        out_specs=pl.BlockSpec((None, n, cb), lambda l, j: (l, 0, j)),
        out_shape=jax.ShapeDtypeStruct((DEPTH, n, 6 * D_MODEL), F32),
        compiler_params=_cparams(("arbitrary", "arbitrary")),
        name="ada",
    )(c_all, w_ada, b_ada.reshape(DEPTH, 1, 6 * D_MODEL))


def _proj_prompt_kernel(x_ref, sh_ref, sc_ref, g_ref, w_ref, pw_ref, ps_ref, cos_ref, sa_ref, sb_ref,
                        ya_ref, q_ref, kv_ref, u_ref, kwin_ref, vwin_ref, pool_ref, xe_ref, s2_ref, s4_ref, s8_ref):
    t = pl.program_id(1)
    tm = x_ref.shape[0]
    hist = POOL_HIST
    last = t == pl.num_programs(1) - 1
    sh = _mod_rows(sh_ref, False)
    sc = _mod_rows(sc_ref, False)
    g1, g2 = POOL_GROUP, 2 * POOL_GROUP

    @pl.when(t == 0)
    def _():
        xe_ref[0:hist, :] = jnp.zeros((hist, W_A), F32)

    for r0 in range(0, tm, PROJ_SUB):
        rows = slice(r0, r0 + PROJ_SUB)
        h = _norm_mod(x_ref[rows, :], g_ref[...], sc, sh).astype(BF16)
        z = _dot(h, w_ref[...])
        xa = z[:, 0:W_A]

        a = hist + r0
        e = a + PROJ_SUB
        xe_ref[a:e, :] = xa
        s2_ref[a - 24:e, :] = xe_ref[a - 24:e, :] + xe_ref[a - 25:e - 1, :]
        s4_ref[a - 16:e, :] = s2_ref[a - 16:e, g1:] + s2_ref[a - 18:e - 2, g1:]
        s8_ref[a - 8:e, :] = s4_ref[a - 8:e, g1:] + s4_ref[a - 12:e - 4, g1:]
        sums = (s2_ref[a:e, 0:g1], s4_ref[a:e, 0:g1], s8_ref[a:e, 0:g1],
                s8_ref[a:e, g1:g2] + s8_ref[a - 8:e - 8, g1:g2])
        pos = t * tm + r0 + lax.broadcasted_iota(jnp.int32, (PROJ_SUB, 1), 0)
        ds = []
        for g, w in enumerate(POOL_WINDOWS):
            cnt = jnp.minimum(pos + 1, w).astype(F32)
            ds.append((sums[g] / cnt - xa[:, g * POOL_GROUP:(g + 1) * POOL_GROUP]).astype(BF16))
        for k in range(2):
            y = _dot(jnp.concatenate(ds[2 * k:2 * k + 2], axis=1), pw_ref[k]) * ps_ref[:, k * g2:(k + 1) * g2]
            ya_ref[rows, k * g2:(k + 1) * g2] = y.astype(BF16)

        cos = cos_ref[rows, :]
        sa = sa_ref[rows, :]
        sb = sb_ref[rows, :]
        for c in range(W_B // LANES):
            qc = z[:, W_A + c * LANES:W_A + (c + 1) * LANES]
            q_ref[rows, c * LANES:(c + 1) * LANES] = (_rope(qc, cos, sa, sb) * (HEAD_DIM ** -0.5)).astype(BF16)
        kz = _rope(z[:, W_A + W_B:W_A + W_B + KV_W], cos, sa, sb)
        vz = z[:, W_A + W_B + KV_W:W_A + W_B + 2 * KV_W]
        kv_ref[rows, 0:128] = kz.astype(BF16)
        kv_ref[rows, 128:256] = pltpu.roll(kz, HEAD_DIM, axis=1).astype(BF16)
        kv_ref[rows, 256:384] = vz.astype(BF16)
        kv_ref[rows, 384:512] = pltpu.roll(vz, HEAD_DIM, axis=1).astype(BF16)
        if r0 + PROJ_SUB == tm:
            @pl.when(last)
            def _(kz=kz, vz=vz):
                kwin_ref[...] = kz[PROJ_SUB - WINDOW:, :]
                vwin_ref[...] = vz[PROJ_SUB - WINDOW:, :]

        u_ref[rows, :] = z[:, W_A + W_B + 2 * KV_W:MIX_COLS]

    @pl.when(last)
    def _():
        pool_ref[...] = xe_ref[hist + tm - POOL_BUF:hist + tm, :]

    xe_ref[0:hist, :] = xe_ref[tm:tm + hist, :]


def _proj_prompt_call(l, x, mod, wts, rope_tabs):
    rows = BATCH * SEQ
    row_blk = lambda b, t: (b * NT + t, 0)
    tab_spec = pl.BlockSpec((TM, LANES), lambda b, t: (t, 0))
    return pl.pallas_call(
        _proj_prompt_kernel,
        grid=(BATCH, NT),
        in_specs=[
            pl.BlockSpec((None, TM, D_MODEL), lambda b, t: (b, t, 0)),
            _mod_spec(l, False, 0), _mod_spec(l, False, 1),
            _layer_spec(l, (1, D_MODEL)),
            _layer_spec(l, (D_MODEL, MIX_COLS)),
            _layer_spec(l, (2, 2 * POOL_GROUP, 2 * POOL_GROUP)),
            _layer_spec(l, (1, W_A)),
            tab_spec, tab_spec, tab_spec,
        ],
        out_specs=[
            pl.BlockSpec((TM, W_A), row_blk),
            pl.BlockSpec((TM, W_B), row_blk),
            pl.BlockSpec((TM, 4 * KV_W), row_blk),
            pl.BlockSpec((TM, W_C), row_blk),
            pl.BlockSpec((None, WINDOW, KV_W), lambda b, t: (b, 0, 0)),
            pl.BlockSpec((None, WINDOW, KV_W), lambda b, t: (b, 0, 0)),
            pl.BlockSpec((None, POOL_BUF, W_A), lambda b, t: (b, 0, 0)),
        ],
        out_shape=[
            jax.ShapeDtypeStruct((rows, W_A), BF16),
            jax.ShapeDtypeStruct((rows, W_B), BF16),
            jax.ShapeDtypeStruct((rows, 4 * KV_W), BF16),
            jax.ShapeDtypeStruct((rows, W_C), F32),
            jax.ShapeDtypeStruct((BATCH, WINDOW, KV_W), F32),
            jax.ShapeDtypeStruct((BATCH, WINDOW, KV_W), F32),
            jax.ShapeDtypeStruct((BATCH, POOL_BUF, W_A), F32),
        ],
        scratch_shapes=[pltpu.VMEM((TM + POOL_HIST, W_A), F32),
                        pltpu.VMEM((TM + POOL_HIST, W_A), F32),
                        pltpu.VMEM((TM + POOL_HIST, W_A - POOL_GROUP), F32),
                        pltpu.VMEM((TM + POOL_HIST, W_A - 2 * POOL_GROUP), F32)],
        compiler_params=_cparams(("arbitrary", "arbitrary")),
        name="proj_prompt",
    )(x, mod, mod, wts["g1"], wts["w_in"], wts["pool_w"], wts["pool_scale"], *rope_tabs)


def _head_pair_operand(lo_mask, a, b):
    zero = jnp.zeros_like(a)
    return jnp.concatenate([jnp.where(lo_mask, a, zero), jnp.where(lo_mask, zero, b)], axis=0)


def _attn_window_block(l, q_ref, yb_ref, sink_ref, r0, kv_prev, kv_cur, prev_bias):
    n = ATT_BLK
    lo = lax.broadcasted_iota(jnp.int32, (n, LANES), 1) < HEAD_DIM
    rows = jnp.bitwise_and(lax.broadcasted_iota(jnp.int32, (2 * n, LANES), 0), n - 1)
    from_prev = lax.broadcasted_iota(jnp.int32, (2 * n, LANES), 1) > rows
    lo2 = lax.broadcasted_iota(jnp.int32, (2 * n, LANES), 1) < HEAD_DIM
    top = lax.broadcasted_iota(jnp.int32, (2 * n, 1), 0) < n
    zero = jnp.zeros((n, LANES), BF16)
    ones_a = jnp.where(lo, 1.0, 0.0).astype(BF16)
    ones_b = jnp.where(lo, 0.0, 1.0).astype(BF16)
    nt_dims = (((1,), (1,)), ((), ()))
    for g in range(KV_HEADS):
        ca, cb = (0, LANES) if g == 0 else (LANES, 0)
        kk = jnp.concatenate([jnp.where(lo, kv_prev[:, ca:ca + LANES], zero),
                              jnp.where(lo, kv_cur[:, ca:ca + LANES], zero),
                              jnp.where(lo, zero, kv_prev[:, cb:cb + LANES]),
                              jnp.where(lo, zero, kv_cur[:, cb:cb + LANES])], axis=0)
        va, vb = 2 * LANES + ca, 2 * LANES + cb
        vv = jnp.concatenate([
            jnp.concatenate([jnp.where(lo, kv_prev[:, va:va + LANES], zero), ones_a], axis=1),
            jnp.concatenate([jnp.where(lo, kv_cur[:, va:va + LANES], zero), ones_a], axis=1),
            jnp.concatenate([jnp.where(lo, zero, kv_prev[:, vb:vb + LANES]), ones_b], axis=1),
            jnp.concatenate([jnp.where(lo, zero, kv_cur[:, vb:vb + LANES]), ones_b], axis=1)], axis=0)
        qs = jnp.concatenate([q_ref[pl.ds(r0, n), (2 * g) * LANES:(2 * g + 1) * LANES],
                              q_ref[pl.ds(r0, n), (2 * g + 1) * LANES:(2 * g + 2) * LANES]], axis=0)
        s = lax.dot_general(qs, kk, nt_dims, preferred_element_type=F32)
        ps, es = [], []
        for hh in range(2):
            s_prev = s[:, hh * 2 * n:hh * 2 * n + n]
            if prev_bias is not None:
                s_prev = s_prev + prev_bias
            logits = jnp.where(from_prev, s_prev, s[:, hh * 2 * n + n:(hh + 1) * 2 * n])
            sk = jnp.where(top, sink_ref[l, 4 * g + hh], sink_ref[l, 4 * g + 2 + hh])
            m = jnp.maximum(jnp.max(logits, axis=-1, keepdims=True), sk)
            p = jnp.exp(logits - m)
            ps += [jnp.where(from_prev, p, 0.0), jnp.where(from_prev, 0.0, p)]
            es.append(jnp.exp(sk - m))
        r = _dot(jnp.concatenate(ps, axis=1).astype(BF16), vv)
        o = r[:, 0:LANES] / (r[:, LANES:2 * LANES] + jnp.where(lo2, es[0], es[1]))
        yb_ref[pl.ds(r0, n), (2 * g) * LANES:(2 * g + 1) * LANES] = o[0:n].astype(BF16)
        yb_ref[pl.ds(r0, n), (2 * g + 1) * LANES:(2 * g + 2) * LANES] = o[n:2 * n].astype(BF16)


def _attn_prompt_kernel(l, sink_ref, q_ref, kvc_ref, kvp_ref, yb_ref):
    i = pl.program_id(1)
    first_bias = jnp.where(i > 0, 0.0, NEG).astype(F32)
    for sub in range(ATT_SUB):
        r0 = sub * ATT_BLK
        kv_cur = kvc_ref[r0:r0 + ATT_BLK, :]
        kv_prev = kvp_ref[...] if sub == 0 else kvc_ref[r0 - ATT_BLK:r0, :]
        _attn_window_block(l, q_ref, yb_ref, sink_ref, r0, kv_prev, kv_cur, first_bias if sub == 0 else None)


def _attn_prompt_call(l, q, kv, sinks):
    rows = ATT_SUB * ATT_BLK
    nb = SEQ // rows
    return pl.pallas_call(
        functools.partial(_attn_prompt_kernel, l),
        grid=(BATCH, nb),
        in_specs=[
            pl.BlockSpec(memory_space=pltpu.SMEM),
            pl.BlockSpec((rows, W_B), lambda b, i: (b * nb + i, 0)),
            pl.BlockSpec((rows, 4 * KV_W), lambda b, i: (b * nb + i, 0)),
            pl.BlockSpec((ATT_BLK, 4 * KV_W), lambda b, i: ((b * nb + i) * ATT_SUB - jnp.minimum(i, 1), 0)),
        ],
        out_specs=pl.BlockSpec((rows, W_B), lambda b, i: (b * nb + i, 0)),
        out_shape=jax.ShapeDtypeStruct((BATCH * SEQ, W_B), BF16),
        compiler_params=_cparams(("arbitrary", "arbitrary")),
        name="attn_prompt",
    )(sinks, q, kv, kv)


def _ssm_kernel(rows_per_step, seq_major, u_ref, h0re_ref, h0im_ref, are_ref, aim_ref, bre_ref, bim_ref, cre_ref,
                cimn_ref, d_ref, wglu_ref, yc_ref, hre_out, him_out, dre, dim_, hre, him, *slabs):
    i = pl.program_id(0)
    r = rows_per_step
    half = N_STATE // 2
    nslab = W_C // LANES
    blk = dre.shape[0]
    steps = SSM_SUB // r

    @pl.when(i == 0)
    def _():
        hre[...] = h0re_ref[...]
        him[...] = h0im_ref[...]

    if seq_major:
        (perm,) = slabs
        tc = u_ref.shape[1]
        for b in range(r):
            for c in range(nslab):
                perm[c, pl.ds(b, tc, stride=r), :] = u_ref[b, :, c * LANES:(c + 1) * LANES]

    def u_rows(rows):
        if seq_major:
            return jnp.concatenate([perm[c, rows, :] for c in range(nslab)], axis=1)
        return u_ref[rows, :]

    def drive(s):
        rows = slice(s * SSM_SUB, (s + 1) * SSM_SUB)
        ub = u_rows(rows).astype(BF16)
        for kt in range(2):
            uk = ub[:, kt * 256:(kt + 1) * 256]
            dre[rows, kt * half:(kt + 1) * half] = _dot(uk, bre_ref[kt])
            dim_[rows, kt * half:(kt + 1) * half] = _dot(uk, bim_ref[kt])

    def scan(s):
        for rt in range(r // SUBLANES):
            rsl = slice(rt * SUBLANES, (rt + 1) * SUBLANES)
            for c in range(N_STATE // SCAN_CW):
                cols = slice(c * SCAN_CW, (c + 1) * SCAN_CW)
                ar = jnp.broadcast_to(are_ref[:, cols], (SUBLANES, SCAN_CW))
                ai = jnp.broadcast_to(aim_ref[:, cols], (SUBLANES, SCAN_CW))
                pr, pi = hre[rsl, cols], him[rsl, cols]
                for t in range(steps):
                    row = s * SSM_SUB + t * r + rt * SUBLANES
                    trow = slice(row, row + SUBLANES)
                    nr = ar * pr - ai * pi + dre[trow, cols]
                    ni = ar * pi + ai * pr + dim_[trow, cols]
                    dre[trow, cols] = nr
                    dim_[trow, cols] = ni
                    pr, pi = nr, ni
                hre[rsl, cols] = pr
                him[rsl, cols] = pi

    def project(s):
        rows = slice(s * SSM_SUB, (s + 1) * SSM_SUB)
        parts = []
        for nt in range(2):
            sl = slice(nt * half, (nt + 1) * half)
            parts.append(_dot(dre[rows, sl].astype(BF16), cre_ref[nt]) + _dot(dim_[rows, sl].astype(BF16), cimn_ref[nt]))
        return jnp.concatenate(parts, axis=1)

    nsub = blk // SSM_SUB
    drive(0)
    ys = []
    for s in range(nsub):
        if s + 1 < nsub:
            drive(s + 1)
        scan(s)
        ys.append(project(s))

    @pl.when(i == pl.num_programs(0) - 1)
    def _():
        hre_out[...] = hre[...]
        him_out[...] = him[...]

    y = jnp.concatenate(ys, axis=0)
    if seq_major:
        for c in range(nslab):
            perm[c] = y[:, c * LANES:(c + 1) * LANES]
        y = jnp.concatenate(
            [jnp.concatenate([perm[c, pl.ds(b, tc, stride=r), :] for c in range(nslab)], axis=1) for b in range(r)],
            axis=0)
        u = u_ref[...].reshape(r * tc, W_C)
    else:
        u = u_ref[...]
    y = jax.nn.gelu(y + d_ref[...] * u)
    yc = (y * jax.nn.sigmoid(_dot(y.astype(BF16), wglu_ref[...]))).astype(BF16)
    yc_ref[...] = yc.reshape(yc_ref.shape)


def _ssm_call(l, u, h0re, h0im, h0_layer, wts, rows_per_step, steps_per_block, seq_major, name):
    blk = rows_per_step * steps_per_block
    half = N_STATE // 2
    if seq_major:
        nsteps = u.shape[1] // steps_per_block
        io_spec = pl.BlockSpec((rows_per_step, steps_per_block, W_C), lambda i: (0, i, 0))
        scratch = [pltpu.VMEM((W_C // LANES, blk, LANES), F32)]
    else:
        nsteps = u.shape[0] // blk
        io_spec = pl.BlockSpec((blk, W_C), lambda i: (i, 0))
        scratch = []
    state_spec = pl.BlockSpec((rows_per_step, N_STATE), lambda i: (0, 0))
    h0_spec = _layer_spec(h0_layer, (rows_per_step, N_STATE))
    return pl.pallas_call(
        functools.partial(_ssm_kernel, rows_per_step, seq_major),
        grid=(nsteps,),
        in_specs=[
            io_spec, h0_spec, h0_spec,
            _layer_spec(l, (1, N_STATE)), _layer_spec(l, (1, N_STATE)),
            _layer_spec(l, (2, 256, half)), _layer_spec(l, (2, 256, half)),
            _layer_spec(l, (2, half, 256)), _layer_spec(l, (2, half, 256)),
            _layer_spec(l, (1, W_C)),
            _layer_spec(l, (W_C, W_C)),
        ],
        out_specs=[io_spec, state_spec, state_spec],
        out_shape=[
            jax.ShapeDtypeStruct(u.shape, BF16),
            jax.ShapeDtypeStruct((rows_per_step, N_STATE), F32),
            jax.ShapeDtypeStruct((rows_per_step, N_STATE), F32),
        ],
        scratch_shapes=[
            pltpu.VMEM((blk, N_STATE), F32),
            pltpu.VMEM((blk, N_STATE), F32),
            pltpu.VMEM((rows_per_step, N_STATE), F32),
            pltpu.VMEM((rows_per_step, N_STATE), F32),
        ] + scratch,
        compiler_params=_cparams(("arbitrary",)),
        name=name,
    )(u, h0re, h0im, wts["ssm_a_re"], wts["ssm_a_im"], wts["ssm_b_re"], wts["ssm_b_im"], wts["ssm_c_re"],
      wts["ssm_c_imn"], wts["ssm_d"], wts["w_glu"])


def _merge_kernel(sample, x_ref, sh_ref, sc_ref, gt_ref, g_ref, ya_ref, yb_ref, yc_ref,
                  wg_ref, wa_ref, wb_ref, wc_ref, wo_ref, o_ref):
    x = x_ref[...]
    h = _norm_mod(x, g_ref[...], _mod_rows(sc_ref, sample), _mod_rows(sh_ref, sample)).astype(BF16)
    if sample:
        t = pl.program_id(1)
        parts = []
        for j in range(N_HEADS // 2):
            start = (j // 2) * 8 + t * 2 + (j % 2)
            parts.append(yb_ref[pl.ds(start, DEC_BATCH, stride=16), :])
        yb = jnp.concatenate(parts, axis=1).astype(BF16)
    else:
        yb = yb_ref[...]
    merged = None
    for k, (y, w_ref) in enumerate(((ya_ref[...], wa_ref), (yb, wb_ref), (yc_ref[...], wc_ref))):
        gate = jax.nn.sigmoid(_dot(h, wg_ref[0, :, k * D_MODEL:(k + 1) * D_MODEL]))
        term = gate * _dot(y, w_ref[...])
        merged = term if merged is None else merged + term
    o_ref[...] = x + _mod_rows(gt_ref, sample) * _dot(merged.astype(BF16), wo_ref[...])


def _row_specs(sample):
    if sample:
        tm = DEC_BATCH
        grid = (1, DEC_SEQ)
        x_spec = pl.BlockSpec((tm, D_MODEL), lambda b, t: (0, t))
        row_spec = lambda cols: pl.BlockSpec((tm, cols), lambda b, t: (t, 0))
    else:
        tm = TM_WIDE
        nt = SEQ // tm
        grid = (BATCH, nt)
        x_spec = pl.BlockSpec((None, tm, D_MODEL), lambda b, t: (b, t, 0))
        row_spec = lambda cols: pl.BlockSpec((tm, cols), lambda b, t: (b * nt + t, 0))
    return grid, x_spec, row_spec


def _merge_call(l, sample, x, mod, ya, yb, yc, wts):
    grid, x_spec, row_spec = _row_specs(sample)
    yb_spec = pl.BlockSpec(yb.shape, lambda b, t: (0, 0)) if sample else row_spec(W_B)
    single = pl.Buffered(1)
    gate_spec = pl.BlockSpec((pl.Element(1), pl.Element(D_MODEL), pl.Element(3 * D_MODEL)),
                             lambda *_: (l, 0, MIX_COLS), pipeline_mode=single)
    return pl.pallas_call(
        functools.partial(_merge_kernel, sample),
        grid=grid,
        in_specs=[
            x_spec, _mod_spec(l, sample, 0), _mod_spec(l, sample, 1), _mod_spec(l, sample, 2),
            _layer_spec(l, (1, D_MODEL)),
            row_spec(W_A), yb_spec, row_spec(W_C),
            gate_spec,
            _layer_spec(l, (W_A, D_MODEL), pipeline_mode=single),
            _layer_spec(l, (W_B, D_MODEL), pipeline_mode=single),
            _layer_spec(l, (W_C, D_MODEL), pipeline_mode=single),
            _layer_spec(l, (D_MODEL, D_MODEL), pipeline_mode=single),
        ],
        out_specs=x_spec,
        out_shape=jax.ShapeDtypeStruct(x.shape, F32),
        compiler_params=_cparams(("arbitrary", "arbitrary")),
        name="merge_sample" if sample else "merge_prompt",
    )(x, mod, mod, mod, wts["g1"], ya, yb, yc, wts["w_in"], wts["w_a"], wts["w_b"], wts["w_c"], wts["w_out"])


def _ffn_kernel(sample, final, x_ref, sh_ref, sc_ref, gt_ref, g_ref, wi_ref, wo_ref, gf_ref, o_ref):
    x = x_ref[...]
    h = _norm_mod(x, g_ref[...], _mod_rows(sc_ref, sample), _mod_rows(sh_ref, sample)).astype(BF16)
    acc = None
    for lo, n in FF_CHUNKS:
        a = _dot(h, wi_ref[:, lo:lo + n])
        b = _dot(h, wi_ref[:, D_FF + lo:D_FF + lo + n])
        act = ((a * jax.nn.sigmoid(a)) * b).astype(BF16)
        part = _dot(act, wo_ref[lo:lo + n, :])
        acc = part if acc is None else acc + part
    y = x + _mod_rows(gt_ref, sample) * acc
    if final:
        r = lax.rsqrt(jnp.mean(y * y, axis=-1, keepdims=True) + EPS)
        y = (y * r) * gf_ref[...]
    o_ref[...] = y


def _ffn_call(l, sample, final, x, mod, wts, g_final):
    grid, x_spec, _ = _row_specs(sample)
    single = pl.Buffered(1)
    return pl.pallas_call(
        functools.partial(_ffn_kernel, sample, final),
        grid=grid,
        in_specs=[
            x_spec, _mod_spec(l, sample, 3), _mod_spec(l, sample, 4), _mod_spec(l, sample, 5),
            _layer_spec(l, (1, D_MODEL)),
            _layer_spec(l, (D_MODEL, 2 * D_FF), pipeline_mode=single),
            _layer_spec(l, (D_FF, D_MODEL), pipeline_mode=single),
            pl.BlockSpec((1, D_MODEL), lambda b, t: (0, 0)),
        ],
        out_specs=x_spec,
        out_shape=jax.ShapeDtypeStruct(x.shape, F32),
        compiler_params=_cparams(("arbitrary", "arbitrary")),
        name="ffn_sample" if sample else "ffn_prompt",
    )(x, mod, mod, mod, wts["g2"], wts["w_ffn_in"], wts["w_ffn_out"], g_final)


def _proj_sample_kernel(x_ref, sh_ref, sc_ref, g_ref, w_ref, pw_ref, ps_ref, cos_ref, sa_ref, sb_ref, pool_prev_ref,
                        ya_ref, q_ref, kn_ref, vn_ref, u_ref, pool_ref):
    nb = DEC_BATCH
    sh = sh_ref[...]
    sc = sc_ref[...]
    h = jnp.concatenate(
        [_norm_mod(x_ref[:, t * D_MODEL:(t + 1) * D_MODEL], g_ref[...], sc, sh) for t in range(DEC_SEQ)],
        axis=0).astype(BF16)

    z = _dot(h, w_ref[...])
    xa = z[:, 0:W_A]
    xe = [pool_prev_ref[:, j * W_A:(j + 1) * W_A] for j in range(POOL_BUF)]
    xe += [xa[t * nb:(t + 1) * nb, :] for t in range(DEC_SEQ)]
    for j in range(POOL_BUF):
        pool_ref[:, j * W_A:(j + 1) * W_A] = xe[DEC_SEQ + j]
    dgs = []
    for g, w in enumerate(POOL_WINDOWS):
        lo = g * POOL_GROUP
        ds = []
        for t in range(DEC_SEQ):
            s = xe[POOL_BUF + t][:, lo:lo + POOL_GROUP]
            for j in range(1, w):
                s = s + xe[POOL_BUF + t - j][:, lo:lo + POOL_GROUP]
            ds.append(s / float(w) - xe[POOL_BUF + t][:, lo:lo + POOL_GROUP])
        dgs.append(jnp.concatenate(ds, axis=0).astype(BF16))
    for k in range(2):
        cols = slice(2 * k * POOL_GROUP, 2 * (k + 1) * POOL_GROUP)
        y = _dot(jnp.concatenate(dgs[2 * k:2 * k + 2], axis=1), pw_ref[k]) * ps_ref[:, cols]
        ya_ref[:, cols] = y.astype(BF16)

    def tabs(t):
        return cos_ref[t:t + 1, :], sa_ref[t:t + 1, :], sb_ref[t:t + 1, :]

    for j in range(W_B // LANES):
        qc = z[:, W_A + j * LANES:W_A + (j + 1) * LANES]
        for t in range(DEC_SEQ):
            qt = _rope(qc[t * nb:(t + 1) * nb, :], *tabs(t)) * (HEAD_DIM ** -0.5)
            q_ref[pl.ds((j // 2) * 8 + t * 2 + (j % 2), nb, stride=16), :] = qt
    kz = z[:, W_A + W_B:W_A + W_B + KV_W]
    vz = z[:, W_A + W_B + KV_W:W_A + W_B + 2 * KV_W]
    for t in range(DEC_SEQ):
        kn_ref[pl.ds(t, nb, stride=DEC_SEQ), :] = _rope(kz[t * nb:(t + 1) * nb, :], *tabs(t))
        vn_ref[pl.ds(t, nb, stride=DEC_SEQ), :] = vz[t * nb:(t + 1) * nb, :]
    u_ref[...] = z[:, W_A + W_B + 2 * KV_W:MIX_COLS]


def _proj_sample_call(l, x2d, mod, wts, rope_tabs, pool_prev):
    rows = DEC_BATCH * DEC_SEQ
    full = lambda shape: pl.BlockSpec(shape, lambda i: (0,) * len(shape))
    out_shape = [
        jax.ShapeDtypeStruct((rows, W_A), BF16),
        jax.ShapeDtypeStruct((DEC_BATCH * 16, LANES), F32),
        jax.ShapeDtypeStruct((rows, KV_W), F32),
        jax.ShapeDtypeStruct((rows, KV_W), F32),
        jax.ShapeDtypeStruct((rows, W_C), F32),
        jax.ShapeDtypeStruct((DEC_BATCH, POOL_BUF * W_A), F32),
    ]
    return pl.pallas_call(
        _proj_sample_kernel,
        grid=(1,),
        in_specs=[
            full(x2d.shape), _mod_spec(l, True, 0), _mod_spec(l, True, 1),
            _layer_spec(l, (1, D_MODEL)),
            _layer_spec(l, (D_MODEL, MIX_COLS)),
            _layer_spec(l, (2, 2 * POOL_GROUP, 2 * POOL_GROUP)),
            _layer_spec(l, (1, W_A)),
            full(rope_tabs[0].shape), full(rope_tabs[1].shape), full(rope_tabs[2].shape),
            _layer_spec(l, (DEC_BATCH, POOL_BUF * W_A)),
        ],
        out_specs=[full(s.shape) for s in out_shape],
        out_shape=out_shape,
        compiler_params=_cparams(("arbitrary",)),
        name="proj_sample",
    )(x2d, mod, mod, wts["g1"], wts["w_in"], wts["pool_w"], wts["pool_scale"], *rope_tabs, pool_prev)


def _attn_sample_kernel(q_ref, kc_ref, vc_ref, kn_ref, vn_ref, sink_ref, yb_ref, ko_ref, vo_ref):
    nn = 2 * SUBLANES
    nprob = KV_HEADS * ATT_SEQS
    nrow = nprob * SUBLANES
    lo_c = lax.broadcasted_iota(jnp.int32, (WINDOW, LANES), 1) < HEAD_DIM
    lo_n = lax.broadcasted_iota(jnp.int32, (SUBLANES, LANES), 1) < HEAD_DIM
    pad = jnp.zeros((SUBLANES - DEC_SEQ, LANES), F32)
    nt_dims = (((1,), (1,)), ((), ()))

    for i in range(ATT_SEQS):
        ko_ref[i, 0:WINDOW - DEC_SEQ, :] = kc_ref[i, DEC_SEQ:WINDOW, :]
        ko_ref[i, WINDOW - DEC_SEQ:WINDOW, :] = kn_ref[i * DEC_SEQ:(i + 1) * DEC_SEQ, :]
        vo_ref[i, 0:WINDOW - DEC_SEQ, :] = vc_ref[i, DEC_SEQ:WINDOW, :]
        vo_ref[i, WINDOW - DEC_SEQ:WINDOW, :] = vn_ref[i * DEC_SEQ:(i + 1) * DEC_SEQ, :]

    def pair_operands(cache, new):
        new8 = jnp.concatenate([new, pad], axis=0)
        cr, nr = pltpu.roll(cache, HEAD_DIM, axis=1), pltpu.roll(new8, HEAD_DIM, axis=1)
        out = []
        for g in range(KV_HEADS):
            (ca, cb), (na, nb) = ((cache, cr), (new8, nr)) if g == 0 else ((cr, cache), (nr, new8))
            out.append(jnp.concatenate([_head_pair_operand(lo_c, ca, cb), _head_pair_operand(lo_n, na, nb)],
                                       axis=0).astype(BF16))
        return out

    kks, vvs = [], []
    for i in range(ATT_SEQS):
        kks += pair_operands(kc_ref[i], kn_ref[i * DEC_SEQ:(i + 1) * DEC_SEQ, :])
        vvs += pair_operands(vc_ref[i], vn_ref[i * DEC_SEQ:(i + 1) * DEC_SEQ, :])

    s = jnp.concatenate(
        [lax.dot_general(q_ref[k * SUBLANES:(k + 1) * SUBLANES, :].astype(BF16), kks[k], nt_dims,
                         preferred_element_type=F32) for k in range(nprob)], axis=0)
    tq_c = jnp.right_shift(jnp.bitwise_and(lax.broadcasted_iota(jnp.int32, (nrow, WINDOW), 0), SUBLANES - 1), 1)
    mask_c = lax.broadcasted_iota(jnp.int32, (nrow, WINDOW), 1) > tq_c
    tq_n = jnp.right_shift(jnp.bitwise_and(lax.broadcasted_iota(jnp.int32, (nrow, nn), 0), SUBLANES - 1), 1)
    col_n = lax.broadcasted_iota(jnp.int32, (nrow, nn), 1)
    vis_n = jnp.bitwise_and(col_n, SUBLANES - 1) <= tq_n
    half_n = jnp.right_shift(col_n, 3)
    s_n = s[:, 2 * WINDOW:2 * WINDOW + nn]
    pcs = []
    pn = jnp.zeros((nrow, nn), F32)
    for hh in range(2):
        sk = sink_ref[:, hh * LANES:hh * LANES + 1]
        sc = jnp.where(mask_c, s[:, hh * WINDOW:(hh + 1) * WINDOW], NEG)
        sn = jnp.where(vis_n & (half_n == hh), s_n, NEG)
        m = jnp.maximum(jnp.maximum(jnp.max(sc, axis=-1, keepdims=True), jnp.max(sn, axis=-1, keepdims=True)), sk)
        pc = jnp.exp(sc - m)
        ph = jnp.exp(sn - m)
        den = jnp.sum(pc, axis=-1, keepdims=True) + jnp.sum(ph, axis=-1, keepdims=True) + jnp.exp(sk - m)
        pcs.append(pc / den)
        pn = pn + ph / den
    p = jnp.concatenate(pcs + [pn], axis=1)
    yb_ref[...] = jnp.concatenate(
        [_dot(p[k * SUBLANES:(k + 1) * SUBLANES, :].astype(BF16), vvs[k]) for k in range(nprob)], axis=0)


def _attn_sample_call(l, q, kc, vc, kn, vn, sink_rows):
    bs = ATT_SEQS
    cache_in = pl.BlockSpec((None, bs, WINDOW, KV_W), lambda i: (l, i, 0, 0))
    cache_out = pl.BlockSpec((bs, WINDOW, KV_W), lambda i: (i, 0, 0))
    new_spec = pl.BlockSpec((bs * DEC_SEQ, KV_W), lambda i: (i, 0))
    q_spec = pl.BlockSpec((bs * 16, LANES), lambda i: (i, 0))
    return pl.pallas_call(
        _attn_sample_kernel,
        grid=(DEC_BATCH // bs,),
        in_specs=[q_spec, cache_in, cache_in, new_spec, new_spec, _layer_spec(l, (bs * 16, 2 * LANES))],
        out_specs=[q_spec, cache_out, cache_out],
        out_shape=[
            jax.ShapeDtypeStruct((DEC_BATCH * 16, LANES), F32),
            jax.ShapeDtypeStruct((DEC_BATCH, WINDOW, KV_W), F32),
            jax.ShapeDtypeStruct((DEC_BATCH, WINDOW, KV_W), F32),
        ],
        compiler_params=_cparams(("arbitrary",)),
        name="attn_sample",
    )(q, kc, vc, kn, vn, sink_rows)


def _rope_tables(pos):
    inv = ROPE_THETA ** (-jnp.arange(0, ROT_DIM, 2, dtype=F32) / ROT_DIM)
    ang = pos.astype(F32)[:, None] * inv[None, :]
    cos, sin = jnp.cos(ang), jnp.sin(ang)
    n = pos.shape[0]
    half = ROT_DIM // 2
    rest = HEAD_DIM - ROT_DIM
    z8 = jnp.zeros((n, half), F32)
    zr = jnp.zeros((n, rest), F32)
    c64 = jnp.concatenate([cos, cos, jnp.ones((n, rest), F32)], axis=1)
    sa64 = jnp.concatenate([z8, sin, zr], axis=1)
    sb64 = jnp.concatenate([-sin, z8, zr], axis=1)
    return tuple(jnp.tile(t, (1, LANES // HEAD_DIM)) for t in (c64, sa64, sb64))


def _ssm_params(a_re, a_im, log_dt, b_re, b_im, c_re, c_im):
    a_re, a_im = a_re.astype(F32), a_im.astype(F32)
    dt = jnp.exp(log_dt.astype(F32))[..., None]
    mag = jnp.exp(a_re * dt)
    abar_re, abar_im = mag * jnp.cos(a_im * dt), mag * jnp.sin(a_im * dt)
    inv = 1.0 / (a_re * a_re + a_im * a_im)
    f_re = (((abar_re - 1.0) * a_re + abar_im * a_im) * inv)[..., None]
    f_im = ((abar_im * a_re - (abar_re - 1.0) * a_im) * inv)[..., None]
    b_re, b_im = b_re.astype(F32), b_im.astype(F32)
    bbar_re = f_re * b_re - f_im * b_im
    bbar_im = f_re * b_im + f_im * b_re
    half = N_STATE // 2
    gpt = SSM_GROUPS // 2

    def diag_tiles(m):
        rr, cc = m.shape[2], m.shape[3]
        rep = jnp.tile(m.reshape(DEPTH, 2, gpt * rr, cc), (1, 1, 1, gpt))
        same = (jnp.arange(gpt * rr)[:, None] // rr) == (jnp.arange(gpt * cc)[None, :] // cc)
        return jnp.where(same, rep, 0.0).astype(BF16)

    diag_in = lambda m: diag_tiles(jnp.swapaxes(m, 2, 3))
    diag_out = lambda m: diag_tiles(jnp.swapaxes(m, 2, 3))

    return dict(
        ssm_a_re=abar_re.reshape(DEPTH, 1, N_STATE), ssm_a_im=abar_im.reshape(DEPTH, 1, N_STATE),
        ssm_b_re=diag_in(bbar_re), ssm_b_im=diag_in(bbar_im),
        ssm_c_re=diag_out(c_re.astype(F32)), ssm_c_imn=diag_out(-c_im.astype(F32)),
    )


def _pool_weight_pairs(pool_w):
    z = jnp.zeros((DEPTH, POOL_GROUP, POOL_GROUP), pool_w.dtype)
    pair = lambda a, b: jnp.concatenate([jnp.concatenate([a, z], axis=2), jnp.concatenate([z, b], axis=2)], axis=1)
    return jnp.stack([pair(pool_w[:, 0], pool_w[:, 1]), pair(pool_w[:, 2], pool_w[:, 3])], axis=1).astype(BF16)


def _sink_rows_sample(sinks):
    pairs = jnp.repeat(sinks.astype(F32).reshape(DEPTH, N_HEADS // 2, 2), LANES, axis=2)
    pairs = pairs.reshape(DEPTH, 1, KV_HEADS, 1, 2, 2 * LANES)
    rows = jnp.broadcast_to(pairs, (DEPTH, ATT_SEQS, KV_HEADS, DEC_SEQ, 2, 2 * LANES))
    return rows.reshape(DEPTH, ATT_SEQS * 16, 2 * LANES)


def kernel(x_prompt, x_sample, cache_win_k, cache_win_v, state_pool, state_ssm_re, state_ssm_im, c_prompt, c_sample, norm1_g, norm2_g, w_ada, b_ada, w_in, pool_w, pool_scale, attn_sinks, ssm_a_re, ssm_a_im, ssm_log_dt, ssm_b_re, ssm_b_im, ssm_c_re, ssm_c_im, ssm_d, w_glu, w_branch_a, w_branch_b, w_branch_c, w_out, w_ffn_in, w_ffn_out, final_norm_g):
    mod = _ada_call(jnp.concatenate([c_sample, c_prompt], axis=0), w_ada, b_ada)
    tabs_p = _rope_tables(jnp.arange(SEQ, dtype=jnp.int32))
    tabs_s = _rope_tables(PAST_LEN + jnp.arange(DEC_SEQ, dtype=jnp.int32))
    g_final = final_norm_g.reshape(1, D_MODEL)

    wts = dict(
        g1=norm1_g.reshape(DEPTH, 1, D_MODEL), g2=norm2_g.reshape(DEPTH, 1, D_MODEL),
        w_in=w_in.astype(BF16),
        pool_w=_pool_weight_pairs(pool_w), pool_scale=pool_scale.reshape(DEPTH, 1, W_A),
        ssm_d=ssm_d.reshape(DEPTH, 1, W_C), w_glu=w_glu.astype(BF16),
        w_a=w_branch_a.astype(BF16), w_b=w_branch_b.astype(BF16), w_c=w_branch_c.astype(BF16),
        w_out=w_out.astype(BF16), w_ffn_in=w_ffn_in.astype(BF16), w_ffn_out=w_ffn_out.astype(BF16),
        **_ssm_params(ssm_a_re, ssm_a_im, ssm_log_dt, ssm_b_re, ssm_b_im, ssm_c_re, ssm_c_im),
    )
    sinks_p = attn_sinks.astype(F32)
    sinks_s = _sink_rows_sample(attn_sinks)
    cache_k = cache_win_k.reshape(DEPTH, DEC_BATCH, WINDOW, KV_W)
    cache_v = cache_win_v.reshape(DEPTH, DEC_BATCH, WINDOW, KV_W)
    pool_prev = state_pool.reshape(DEPTH, DEC_BATCH, POOL_BUF * W_A)
    h0_re = state_ssm_re.reshape(DEPTH, DEC_BATCH, N_STATE)
    h0_im = state_ssm_im.reshape(DEPTH, DEC_BATCH, N_STATE)
    zeros_state = jnp.zeros((1, BATCH, N_STATE), F32)

    hp = x_prompt
    hs = x_sample.reshape(DEC_BATCH, DEC_SEQ * D_MODEL)
    outs = [[] for _ in range(10)]
    for l in range(DEPTH):
        last = l == DEPTH - 1

        ya, q, kv, u, kwin, vwin, pool_new = _proj_prompt_call(l, hp, mod, wts, tabs_p)
        yb = _attn_prompt_call(l, q, kv, sinks_p)
        yc, hre, him = _ssm_call(l, u.reshape(BATCH, SEQ, W_C), zeros_state, zeros_state, 0, wts,
                                 BATCH, SSM_TC, True, "ssm_prompt")
        hp = _merge_call(l, False, hp, mod, ya, yb, yc.reshape(BATCH * SEQ, W_C), wts)
        hp = _ffn_call(l, False, last, hp, mod, wts, g_final)
        for k, v in zip(range(5), (kwin.reshape(BATCH, WINDOW, KV_HEADS, HEAD_DIM),
                                   vwin.reshape(BATCH, WINDOW, KV_HEADS, HEAD_DIM), pool_new,
                                   hre.reshape(BATCH, SSM_GROUPS, SSM_STATE),
                                   him.reshape(BATCH, SSM_GROUPS, SSM_STATE))):
            outs[k].append(v)

        ya, q, kn, vn, u, pool_new = _proj_sample_call(l, hs, mod, wts, tabs_s, pool_prev)
        yb, k_new, v_new = _attn_sample_call(l, q, cache_k, cache_v, kn, vn, sinks_s)
        yc, hre, him = _ssm_call(l, u, h0_re, h0_im, l, wts, DEC_BATCH, DEC_SEQ, False, "ssm_sample")
        hs = _merge_call(l, True, hs, mod, ya, yb, yc, wts)
        hs = _ffn_call(l, True, last, hs, mod, wts, g_final)
        for k, v in zip(range(5, 10), (k_new.reshape(DEC_BATCH, WINDOW, KV_HEADS, HEAD_DIM),
                                       v_new.reshape(DEC_BATCH, WINDOW, KV_HEADS, HEAD_DIM),
                                       pool_new.reshape(DEC_BATCH, POOL_BUF, W_A),
                                       hre.reshape(DEC_BATCH, SSM_GROUPS, SSM_STATE),
                                       him.reshape(DEC_BATCH, SSM_GROUPS, SSM_STATE))):
            outs[k].append(v)

    return (hp, hs.reshape(DEC_BATCH, DEC_SEQ, D_MODEL), *[jnp.stack(o) for o in outs])
```

```python
import functools

import jax
import jax.numpy as jnp
from jax import lax
from jax.experimental import pallas as pl
from jax.experimental.pallas import tpu as pltpu

F32 = jnp.float32
BF16 = jnp.bfloat16

D_MODEL = 1024
BATCH = 8
SEQ = 2048
DEPTH = 4
DEC_BATCH = 128
DEC_SEQ = 4
PAST_LEN = 8192

W_A = 512
POOL_WINDOWS = (2, 4, 8, 16)
POOL_GROUP = 128
POOL_BUF = 15
POOL_HIST = 32
HEAD_DIM = 64
N_HEADS = 8
KV_HEADS = 2
W_B = 512
KV_W = 128
WINDOW = 128
ROT_DIM = 16
ROPE_THETA = 500000.0
W_C = 512
SSM_CH = 16
SSM_GROUPS = 32
SSM_STATE = 64
N_STATE = SSM_GROUPS * SSM_STATE
D_FF = 2816
EPS = 1e-6
MIX_COLS = W_A + W_B + 2 * KV_W + W_C

LANES = 128
SUBLANES = 8
VMEM_LIMIT = 56 * 1024 * 1024

TM = 1024
NT = SEQ // TM
PROJ_SUB = 128
ATT_BLK = WINDOW
ATT_SUB = 8
TM_WIDE = 1024
ATT_SEQS = 16
SSM_TC = 128
SSM_SUB = 128
SCAN_CW = 512
NEG = -1e30
FF_CHUNKS = ((0, 1024), (1024, 1024), (2048, 768))
PROMPT_MOD_BLOCK = DEC_BATCH // BATCH


def _cparams(sem):
    return pltpu.CompilerParams(dimension_semantics=sem, vmem_limit_bytes=VMEM_LIMIT)


def _dot(a, b):
    return jnp.dot(a, b, preferred_element_type=F32)


def _norm_mod(x, g, sc, sh):
    r = lax.rsqrt(jnp.mean(x * x, axis=-1, keepdims=True) + EPS)
    return (x * r) * g * (1.0 + sc) + sh


def _mod_rows(ref, sample):
    return ref[...] if sample else ref[pl.ds(pl.program_id(0), 1), :]


def _rope(x, cos, sa, sb):
    return x * cos + pltpu.roll(x, 8, axis=1) * sa + pltpu.roll(x, LANES - 8, axis=1) * sb


def _layer_spec(l, tail, **kw):
    nz = (0,) * len(tail)
    return pl.BlockSpec((None, *tail), lambda *_: (l, *nz), **kw)


def _mod_spec(l, sample, j):
    if sample:
        return pl.BlockSpec((None, DEC_BATCH, D_MODEL), lambda *_: (l, 0, j))
    return pl.BlockSpec((None, BATCH, D_MODEL), lambda *_: (l, PROMPT_MOD_BLOCK, j))


def _ada_kernel(c_ref, w_ref, b_ref, o_ref):
    c = c_ref[...]
    a = (c * jax.nn.sigmoid(c)).astype(BF16)
    o_ref[...] = _dot(a, w_ref[...].astype(BF16)) + b_ref[...]


def _ada_call(c_all, w_ada, b_ada):
    n = c_all.shape[0]
    cb = 1536
    return pl.pallas_call(
        _ada_kernel,
        grid=(DEPTH, 6 * D_MODEL // cb),
        in_specs=[
            pl.BlockSpec((n, D_MODEL), lambda l, j: (0, 0)),
            pl.BlockSpec((None, D_MODEL, cb), lambda l, j: (l, 0, j)),
            pl.BlockSpec((None, 1, cb), lambda l, j: (l, 0, j)),
        ],
        out_specs=pl.BlockSpec((None, n, cb), lambda l, j: (l, 0, j)),
        out_shape=jax.ShapeDtypeStruct((DEPTH, n, 6 * D_MODEL), F32),
        compiler_params=_cparams(("arbitrary", "arbitrary")),
        name="ada",
    )(c_all, w_ada, b_ada.reshape(DEPTH, 1, 6 * D_MODEL))


def _proj_prompt_kernel(x_ref, sh_ref, sc_ref, g_ref, w_ref, pw_ref, ps_ref, cos_ref, sa_ref, sb_ref,
                        ya_ref, q_ref, kv_ref, u_ref, kwin_ref, vwin_ref, pool_ref, xe_ref, s2_ref, s4_ref, s8_ref):
    t = pl.program_id(1)
    tm = x_ref.shape[0]
    hist = POOL_HIST
    last = t == pl.num_programs(1) - 1
    sh = _mod_rows(sh_ref, False)
    sc = _mod_rows(sc_ref, False)
    g1, g2 = POOL_GROUP, 2 * POOL_GROUP

    @pl.when(t == 0)
    def _():
        xe_ref[0:hist, :] = jnp.zeros((hist, W_A), F32)

    for r0 in range(0, tm, PROJ_SUB):
        rows = slice(r0, r0 + PROJ_SUB)
        h = _norm_mod(x_ref[rows, :], g_ref[...], sc, sh).astype(BF16)
        z = _dot(h, w_ref[...])
        xa = z[:, 0:W_A]

        a = hist + r0
        e = a + PROJ_SUB
        xe_ref[a:e, :] = xa
        s2_ref[a - 24:e, :] = xe_ref[a - 24:e, :] + xe_ref[a - 25:e - 1, :]
        s4_ref[a - 16:e, :] = s2_ref[a - 16:e, g1:] + s2_ref[a - 18:e - 2, g1:]
        s8_ref[a - 8:e, :] = s4_ref[a - 8:e, g1:] + s4_ref[a - 12:e - 4, g1:]
        sums = (s2_ref[a:e, 0:g1], s4_ref[a:e, 0:g1], s8_ref[a:e, 0:g1],
                s8_ref[a:e, g1:g2] + s8_ref[a - 8:e - 8, g1:g2])
        pos = t * tm + r0 + lax.broadcasted_iota(jnp.int32, (PROJ_SUB, 1), 0)
        ds = []
        for g, w in enumerate(POOL_WINDOWS):
            cnt = jnp.minimum(pos + 1, w).astype(F32)
            ds.append((sums[g] / cnt - xa[:, g * POOL_GROUP:(g + 1) * POOL_GROUP]).astype(BF16))
        for k in range(2):
            y = _dot(jnp.concatenate(ds[2 * k:2 * k + 2], axis=1), pw_ref[k]) * ps_ref[:, k * g2:(k + 1) * g2]
            ya_ref[rows, k * g2:(k + 1) * g2] = y.astype(BF16)

        cos = cos_ref[rows, :]
        sa = sa_ref[rows, :]
        sb = sb_ref[rows, :]
        for c in range(W_B // LANES):
            qc = z[:, W_A + c * LANES:W_A + (c + 1) * LANES]
            q_ref[rows, c * LANES:(c + 1) * LANES] = (_rope(qc, cos, sa, sb) * (HEAD_DIM ** -0.5)).astype(BF16)
        kz = _rope(z[:, W_A + W_B:W_A + W_B + KV_W], cos, sa, sb)
        vz = z[:, W_A + W_B + KV_W:W_A + W_B + 2 * KV_W]
        kv_ref[rows, 0:128] = kz.astype(BF16)
        kv_ref[rows, 128:256] = pltpu.roll(kz, HEAD_DIM, axis=1).astype(BF16)
        kv_ref[rows, 256:384] = vz.astype(BF16)
        kv_ref[rows, 384:512] = pltpu.roll(vz, HEAD_DIM, axis=1).astype(BF16)
        if r0 + PROJ_SUB == tm:
            @pl.when(last)
            def _(kz=kz, vz=vz):
                kwin_ref[...] = kz[PROJ_SUB - WINDOW:, :]
                vwin_ref[...] = vz[PROJ_SUB - WINDOW:, :]

        u_ref[rows, :] = z[:, W_A + W_B + 2 * KV_W:MIX_COLS]

    @pl.when(last)
    def _():
        pool_ref[...] = xe_ref[hist + tm - POOL_BUF:hist + tm, :]

    xe_ref[0:hist, :] = xe_ref[tm:tm + hist, :]


def _proj_prompt_call(l, x, mod, wts, rope_tabs):
    rows = BATCH * SEQ
    row_blk = lambda b, t: (b * NT + t, 0)
    tab_spec = pl.BlockSpec((TM, LANES), lambda b, t: (t, 0))
    return pl.pallas_call(
        _proj_prompt_kernel,
        grid=(BATCH, NT),
        in_specs=[
            pl.BlockSpec((None, TM, D_MODEL), lambda b, t: (b, t, 0)),
            _mod_spec(l, False, 0), _mod_spec(l, False, 1),
            _layer_spec(l, (1, D_MODEL)),
            _layer_spec(l, (D_MODEL, MIX_COLS)),
            _layer_spec(l, (2, 2 * POOL_GROUP, 2 * POOL_GROUP)),
            _layer_spec(l, (1, W_A)),
            tab_spec, tab_spec, tab_spec,
        ],
        out_specs=[
            pl.BlockSpec((TM, W_A), row_blk),
            pl.BlockSpec((TM, W_B), row_blk),
            pl.BlockSpec((TM, 4 * KV_W), row_blk),
            pl.BlockSpec((TM, W_C), row_blk),
            pl.BlockSpec((None, WINDOW, KV_W), lambda b, t: (b, 0, 0)),
            pl.BlockSpec((None, WINDOW, KV_W), lambda b, t: (b, 0, 0)),
            pl.BlockSpec((None, POOL_BUF, W_A), lambda b, t: (b, 0, 0)),
        ],
        out_shape=[
            jax.ShapeDtypeStruct((rows, W_A), BF16),
            jax.ShapeDtypeStruct((rows, W_B), BF16),
            jax.ShapeDtypeStruct((rows, 4 * KV_W), BF16),
            jax.ShapeDtypeStruct((rows, W_C), F32),
            jax.ShapeDtypeStruct((BATCH, WINDOW, KV_W), F32),
            jax.ShapeDtypeStruct((BATCH, WINDOW, KV_W), F32),
            jax.ShapeDtypeStruct((BATCH, POOL_BUF, W_A), F32),
        ],
        scratch_shapes=[pltpu.VMEM((TM + POOL_HIST, W_A), F32),
                        pltpu.VMEM((TM + POOL_HIST, W_A), F32),
                        pltpu.VMEM((TM + POOL_HIST, W_A - POOL_GROUP), F32),
                        pltpu.VMEM((TM + POOL_HIST, W_A - 2 * POOL_GROUP), F32)],
        compiler_params=_cparams(("arbitrary", "arbitrary")),
        name="proj_prompt",
    )(x, mod, mod, wts["g1"], wts["w_in"], wts["pool_w"], wts["pool_scale"], *rope_tabs)


def _head_pair_operand(lo_mask, a, b):
    zero = jnp.zeros_like(a)
    return jnp.concatenate([jnp.where(lo_mask, a, zero), jnp.where(lo_mask, zero, b)], axis=0)


def _attn_window_block(l, q_ref, yb_ref, sink_ref, r0, kv_prev, kv_cur, prev_bias):
    n = ATT_BLK
    lo = lax.broadcasted_iota(jnp.int32, (n, LANES), 1) < HEAD_DIM
    rows = jnp.bitwise_and(lax.broadcasted_iota(jnp.int32, (2 * n, LANES), 0), n - 1)
    from_prev = lax.broadcasted_iota(jnp.int32, (2 * n, LANES), 1) > rows
    lo2 = lax.broadcasted_iota(jnp.int32, (2 * n, LANES), 1) < HEAD_DIM
    top = lax.broadcasted_iota(jnp.int32, (2 * n, 1), 0) < n
    zero = jnp.zeros((n, LANES), BF16)
    ones_a = jnp.where(lo, 1.0, 0.0).astype(BF16)
    ones_b = jnp.where(lo, 0.0, 1.0).astype(BF16)
    nt_dims = (((1,), (1,)), ((), ()))
    for g in range(KV_HEADS):
        ca, cb = (0, LANES) if g == 0 else (LANES, 0)
        kk = jnp.concatenate([jnp.where(lo, kv_prev[:, ca:ca + LANES], zero),
                              jnp.where(lo, kv_cur[:, ca:ca + LANES], zero),
                              jnp.where(lo, zero, kv_prev[:, cb:cb + LANES]),
                              jnp.where(lo, zero, kv_cur[:, cb:cb + LANES])], axis=0)
        va, vb = 2 * LANES + ca, 2 * LANES + cb
        vv = jnp.concatenate([
            jnp.concatenate([jnp.where(lo, kv_prev[:, va:va + LANES], zero), ones_a], axis=1),
            jnp.concatenate([jnp.where(lo, kv_cur[:, va:va + LANES], zero), ones_a], axis=1),
            jnp.concatenate([jnp.where(lo, zero, kv_prev[:, vb:vb + LANES]), ones_b], axis=1),
            jnp.concatenate([jnp.where(lo, zero, kv_cur[:, vb:vb + LANES]), ones_b], axis=1)], axis=0)
        qs = jnp.concatenate([q_ref[pl.ds(r0, n), (2 * g) * LANES:(2 * g + 1) * LANES],
                              q_ref[pl.ds(r0, n), (2 * g + 1) * LANES:(2 * g + 2) * LANES]], axis=0)
        s = lax.dot_general(qs, kk, nt_dims, preferred_element_type=F32)
        ps, es = [], []
        for hh in range(2):
            s_prev = s[:, hh * 2 * n:hh * 2 * n + n]
            if prev_bias is not None:
                s_prev = s_prev + prev_bias
            logits = jnp.where(from_prev, s_prev, s[:, hh * 2 * n + n:(hh + 1) * 2 * n])
            sk = jnp.where(top, sink_ref[l, 4 * g + hh], sink_ref[l, 4 * g + 2 + hh])
            m = jnp.maximum(jnp.max(logits, axis=-1, keepdims=True), sk)
            p = jnp.exp(logits - m)
            ps += [jnp.where(from_prev, p, 0.0), jnp.where(from_prev, 0.0, p)]
            es.append(jnp.exp(sk - m))
        r = _dot(jnp.concatenate(ps, axis=1).astype(BF16), vv)
        o = r[:, 0:LANES] / (r[:, LANES:2 * LANES] + jnp.where(lo2, es[0], es[1]))
        yb_ref[pl.ds(r0, n), (2 * g) * LANES:(2 * g + 1) * LANES] = o[0:n].astype(BF16)
        yb_ref[pl.ds(r0, n), (2 * g + 1) * LANES:(2 * g + 2) * LANES] = o[n:2 * n].astype(BF16)


def _attn_prompt_kernel(l, sink_ref, q_ref, kvc_ref, kvp_ref, yb_ref):
    i = pl.program_id(1)
    first_bias = jnp.where(i > 0, 0.0, NEG).astype(F32)
    for sub in range(ATT_SUB):
        r0 = sub * ATT_BLK
        kv_cur = kvc_ref[r0:r0 + ATT_BLK, :]
        kv_prev = kvp_ref[...] if sub == 0 else kvc_ref[r0 - ATT_BLK:r0, :]
        _attn_window_block(l, q_ref, yb_ref, sink_ref, r0, kv_prev, kv_cur, first_bias if sub == 0 else None)


def _attn_prompt_call(l, q, kv, sinks):
    rows = ATT_SUB * ATT_BLK
    nb = SEQ // rows
    return pl.pallas_call(
        functools.partial(_attn_prompt_kernel, l),
        grid=(BATCH, nb),
        in_specs=[
            pl.BlockSpec(memory_space=pltpu.SMEM),
            pl.BlockSpec((rows, W_B), lambda b, i: (b * nb + i, 0)),
            pl.BlockSpec((rows, 4 * KV_W), lambda b, i: (b * nb + i, 0)),
            pl.BlockSpec((ATT_BLK, 4 * KV_W), lambda b, i: ((b * nb + i) * ATT_SUB - jnp.minimum(i, 1), 0)),
        ],
        out_specs=pl.BlockSpec((rows, W_B), lambda b, i: (b * nb + i, 0)),
        out_shape=jax.ShapeDtypeStruct((BATCH * SEQ, W_B), BF16),
        compiler_params=_cparams(("arbitrary", "arbitrary")),
        name="attn_prompt",
    )(sinks, q, kv, kv)


def _ssm_kernel(rows_per_step, seq_major, u_ref, h0re_ref, h0im_ref, are_ref, aim_ref, bre_ref, bim_ref, cre_ref,
                cimn_ref, d_ref, wglu_ref, yc_ref, hre_out, him_out, dre, dim_, hre, him, *slabs):
    i = pl.program_id(0)
    r = rows_per_step
    half = N_STATE // 2
    nslab = W_C // LANES
    blk = dre.shape[0]
    steps = SSM_SUB // r

    @pl.when(i == 0)
    def _():
        hre[...] = h0re_ref[...]
        him[...] = h0im_ref[...]

    if seq_major:
        (perm,) = slabs
        tc = u_ref.shape[1]
        for b in range(r):
            for c in range(nslab):
                perm[c, pl.ds(b, tc, stride=r), :] = u_ref[b, :, c * LANES:(c + 1) * LANES]

    def u_rows(rows):
        if seq_major:
            return jnp.concatenate([perm[c, rows, :] for c in range(nslab)], axis=1)
        return u_ref[rows, :]

    def drive(s):
        rows = slice(s * SSM_SUB, (s + 1) * SSM_SUB)
        ub = u_rows(rows).astype(BF16)
        for kt in range(2):
            uk = ub[:, kt * 256:(kt + 1) * 256]
            dre[rows, kt * half:(kt + 1) * half] = _dot(uk, bre_ref[kt])
            dim_[rows, kt * half:(kt + 1) * half] = _dot(uk, bim_ref[kt])

    def scan(s):
        for rt in range(r // SUBLANES):
            rsl = slice(rt * SUBLANES, (rt + 1) * SUBLANES)
            for c in range(N_STATE // SCAN_CW):
                cols = slice(c * SCAN_CW, (c + 1) * SCAN_CW)
                ar = jnp.broadcast_to(are_ref[:, cols], (SUBLANES, SCAN_CW))
                ai = jnp.broadcast_to(aim_ref[:, cols], (SUBLANES, SCAN_CW))
                pr, pi = hre[rsl, cols], him[rsl, cols]
                for t in range(steps):
                    row = s * SSM_SUB + t * r + rt * SUBLANES
                    trow = slice(row, row + SUBLANES)
                    nr = ar * pr - ai * pi + dre[trow, cols]
                    ni = ar * pi + ai * pr + dim_[trow, cols]
                    dre[trow, cols] = nr
                    dim_[trow, cols] = ni
                    pr, pi = nr, ni
                hre[rsl, cols] = pr
                him[rsl, cols] = pi

    def project(s):
        rows = slice(s * SSM_SUB, (s + 1) * SSM_SUB)
        parts = []
        for nt in range(2):
            sl = slice(nt * half, (nt + 1) * half)
            parts.append(_dot(dre[rows, sl].astype(BF16), cre_ref[nt]) + _dot(dim_[rows, sl].astype(BF16), cimn_ref[nt]))
        return jnp.concatenate(parts, axis=1)

    nsub = blk // SSM_SUB
    drive(0)
    ys = []
    for s in range(nsub):
        if s + 1 < nsub:
            drive(s + 1)
        scan(s)
        ys.append(project(s))

    @pl.when(i == pl.num_programs(0) - 1)
    def _():
        hre_out[...] = hre[...]
        him_out[...] = him[...]

    y = jnp.concatenate(ys, axis=0)
    if seq_major:
        for c in range(nslab):
            perm[c] = y[:, c * LANES:(c + 1) * LANES]
        y = jnp.concatenate(
            [jnp.concatenate([perm[c, pl.ds(b, tc, stride=r), :] for c in range(nslab)], axis=1) for b in range(r)],
            axis=0)
        u = u_ref[...].reshape(r * tc, W_C)
    else:
        u = u_ref[...]
    y = jax.nn.gelu(y + d_ref[...] * u)
    yc = (y * jax.nn.sigmoid(_dot(y.astype(BF16), wglu_ref[...]))).astype(BF16)
    yc_ref[...] = yc.reshape(yc_ref.shape)


def _ssm_call(l, u, h0re, h0im, h0_layer, wts, rows_per_step, steps_per_block, seq_major, name):
    blk = rows_per_step * steps_per_block
    half = N_STATE // 2
    if seq_major:
        nsteps = u.shape[1] // steps_per_block
        io_spec = pl.BlockSpec((rows_per_step, steps_per_block, W_C), lambda i: (0, i, 0))
        scratch = [pltpu.VMEM((W_C // LANES, blk, LANES), F32)]
    else:
        nsteps = u.shape[0] // blk
        io_spec = pl.BlockSpec((blk, W_C), lambda i: (i, 0))
        scratch = []
    state_spec = pl.BlockSpec((rows_per_step, N_STATE), lambda i: (0, 0))
    h0_spec = _layer_spec(h0_layer, (rows_per_step, N_STATE))
    return pl.pallas_call(
        functools.partial(_ssm_kernel, rows_per_step, seq_major),
        grid=(nsteps,),
        in_specs=[
            io_spec, h0_spec, h0_spec,
            _layer_spec(l, (1, N_STATE)), _layer_spec(l, (1, N_STATE)),
            _layer_spec(l, (2, 256, half)), _layer_spec(l, (2, 256, half)),
            _layer_spec(l, (2, half, 256)), _layer_spec(l, (2, half, 256)),
            _layer_spec(l, (1, W_C)),
            _layer_spec(l, (W_C, W_C)),
        ],
        out_specs=[io_spec, state_spec, state_spec],
        out_shape=[
            jax.ShapeDtypeStruct(u.shape, BF16),
            jax.ShapeDtypeStruct((rows_per_step, N_STATE), F32),
            jax.ShapeDtypeStruct((rows_per_step, N_STATE), F32),
        ],
        scratch_shapes=[
            pltpu.VMEM((blk, N_STATE), F32),
            pltpu.VMEM((blk, N_STATE), F32),
            pltpu.VMEM((rows_per_step, N_STATE), F32),
            pltpu.VMEM((rows_per_step, N_STATE), F32),
        ] + scratch,
        compiler_params=_cparams(("arbitrary",)),
        name=name,
    )(u, h0re, h0im, wts["ssm_a_re"], wts["ssm_a_im"], wts["ssm_b_re"], wts["ssm_b_im"], wts["ssm_c_re"],
      wts["ssm_c_imn"], wts["ssm_d"], wts["w_glu"])


def _merge_kernel(sample, x_ref, sh_ref, sc_ref, gt_ref, g_ref, ya_ref, yb_ref, yc_ref,
                  wg_ref, wa_ref, wb_ref, wc_ref, wo_ref, o_ref):
    x = x_ref[...]
    h = _norm_mod(x, g_ref[...], _mod_rows(sc_ref, sample), _mod_rows(sh_ref, sample)).astype(BF16)
    if sample:
        t = pl.program_id(1)
        parts = []
        for j in range(N_HEADS // 2):
            start = (j // 2) * 8 + t * 2 + (j % 2)
            parts.append(yb_ref[pl.ds(start, DEC_BATCH, stride=16), :])
        yb = jnp.concatenate(parts, axis=1).astype(BF16)
    else:
        yb = yb_ref[...]
    merged = None
    for k, (y, w_ref) in enumerate(((ya_ref[...], wa_ref), (yb, wb_ref), (yc_ref[...], wc_ref))):
        gate = jax.nn.sigmoid(_dot(h, wg_ref[0, :, k * D_MODEL:(k + 1) * D_MODEL]))
        term = gate * _dot(y, w_ref[...])
        merged = term if merged is None else merged + term
    o_ref[...] = x + _mod_rows(gt_ref, sample) * _dot(merged.astype(BF16), wo_ref[...])


def _row_specs(sample):
    if sample:
        tm = DEC_BATCH
        grid = (1, DEC_SEQ)
        x_spec = pl.BlockSpec((tm, D_MODEL), lambda b, t: (0, t))
        row_spec = lambda cols: pl.BlockSpec((tm, cols), lambda b, t: (t, 0))
    else:
        tm = TM_WIDE
        nt = SEQ // tm
        grid = (BATCH, nt)
        x_spec = pl.BlockSpec((None, tm, D_MODEL), lambda b, t: (b, t, 0))
        row_spec = lambda cols: pl.BlockSpec((tm, cols), lambda b, t: (b * nt + t, 0))
    return grid, x_spec, row_spec


def _merge_call(l, sample, x, mod, ya, yb, yc, wts):
    grid, x_spec, row_spec = _row_specs(sample)
    yb_spec = pl.BlockSpec(yb.shape, lambda b, t: (0, 0)) if sample else row_spec(W_B)
    single = pl.Buffered(1)
    gate_spec = pl.BlockSpec((pl.Element(1), pl.Element(D_MODEL), pl.Element(3 * D_MODEL)),
                             lambda *_: (l, 0, MIX_COLS), pipeline_mode=single)
    return pl.pallas_call(
        functools.partial(_merge_kernel, sample),
        grid=grid,
        in_specs=[
            x_spec, _mod_spec(l, sample, 0), _mod_spec(l, sample, 1), _mod_spec(l, sample, 2),
            _layer_spec(l, (1, D_MODEL)),
            row_spec(W_A), yb_spec, row_spec(W_C),
            gate_spec,
            _layer_spec(l, (W_A, D_MODEL), pipeline_mode=single),
            _layer_spec(l, (W_B, D_MODEL), pipeline_mode=single),
            _layer_spec(l, (W_C, D_MODEL), pipeline_mode=single),
            _layer_spec(l, (D_MODEL, D_MODEL), pipeline_mode=single),
        ],
        out_specs=x_spec,
        out_shape=jax.ShapeDtypeStruct(x.shape, F32),
        compiler_params=_cparams(("arbitrary", "arbitrary")),
        name="merge_sample" if sample else "merge_prompt",
    )(x, mod, mod, mod, wts["g1"], ya, yb, yc, wts["w_in"], wts["w_a"], wts["w_b"], wts["w_c"], wts["w_out"])


def _ffn_kernel(sample, final, x_ref, sh_ref, sc_ref, gt_ref, g_ref, wi_ref, wo_ref, gf_ref, o_ref):
    x = x_ref[...]
    h = _norm_mod(x, g_ref[...], _mod_rows(sc_ref, sample), _mod_rows(sh_ref, sample)).astype(BF16)
    acc = None
    for lo, n in FF_CHUNKS:
        a = _dot(h, wi_ref[:, lo:lo + n])
        b = _dot(h, wi_ref[:, D_FF + lo:D_FF + lo + n])
        act = ((a * jax.nn.sigmoid(a)) * b).astype(BF16)
        part = _dot(act, wo_ref[lo:lo + n, :])
        acc = part if acc is None else acc + part
    y = x + _mod_rows(gt_ref, sample) * acc
    if final:
        r = lax.rsqrt(jnp.mean(y * y, axis=-1, keepdims=True) + EPS)
        y = (y * r) * gf_ref[...]
    o_ref[...] = y


def _ffn_call(l, sample, final, x, mod, wts, g_final):
    grid, x_spec, _ = _row_specs(sample)
    single = pl.Buffered(1)
    return pl.pallas_call(
        functools.partial(_ffn_kernel, sample, final),
        grid=grid,
        in_specs=[
            x_spec, _mod_spec(l, sample, 3), _mod_spec(l, sample, 4), _mod_spec(l, sample, 5),
            _layer_spec(l, (1, D_MODEL)),
            _layer_spec(l, (D_MODEL, 2 * D_FF), pipeline_mode=single),
            _layer_spec(l, (D_FF, D_MODEL), pipeline_mode=single),
            pl.BlockSpec((1, D_MODEL), lambda b, t: (0, 0)),
        ],
        out_specs=x_spec,
        out_shape=jax.ShapeDtypeStruct(x.shape, F32),
        compiler_params=_cparams(("arbitrary", "arbitrary")),
        name="ffn_sample" if sample else "ffn_prompt",
    )(x, mod, mod, mod, wts["g2"], wts["w_ffn_in"], wts["w_ffn_out"], g_final)


def _proj_sample_kernel(x_ref, sh_ref, sc_ref, g_ref, w_ref, pw_ref, ps_ref, cos_ref, sa_ref, sb_ref, pool_prev_ref,
                        ya_ref, q_ref, kn_ref, vn_ref, u_ref, pool_ref):
    nb = DEC_BATCH
    sh = sh_ref[...]
    sc = sc_ref[...]
    h = jnp.concatenate(
        [_norm_mod(x_ref[:, t * D_MODEL:(t + 1) * D_MODEL], g_ref[...], sc, sh) for t in range(DEC_SEQ)],
        axis=0).astype(BF16)

    z = _dot(h, w_ref[...])
    xa = z[:, 0:W_A]
    xe = [pool_prev_ref[:, j * W_A:(j + 1) * W_A] for j in range(POOL_BUF)]
    xe += [xa[t * nb:(t + 1) * nb, :] for t in range(DEC_SEQ)]
    for j in range(POOL_BUF):
        pool_ref[:, j * W_A:(j + 1) * W_A] = xe[DEC_SEQ + j]
    dgs = []
    for g, w in enumerate(POOL_WINDOWS):
        lo = g * POOL_GROUP
        ds = []
        for t in range(DEC_SEQ):
            s = xe[POOL_BUF + t][:, lo:lo + POOL_GROUP]
            for j in range(1, w):
                s = s + xe[POOL_BUF + t - j][:, lo:lo + POOL_GROUP]
            ds.append(s / float(w) - xe[POOL_BUF + t][:, lo:lo + POOL_GROUP])
        dgs.append(jnp.concatenate(ds, axis=0).astype(BF16))
    for k in range(2):
        cols = slice(2 * k * POOL_GROUP, 2 * (k + 1) * POOL_GROUP)
        y = _dot(jnp.concatenate(dgs[2 * k:2 * k + 2], axis=1), pw_ref[k]) * ps_ref[:, cols]
        ya_ref[:, cols] = y.astype(BF16)

    def tabs(t):
        return cos_ref[t:t + 1, :], sa_ref[t:t + 1, :], sb_ref[t:t + 1, :]

    for j in range(W_B // LANES):
        qc = z[:, W_A + j * LANES:W_A + (j + 1) * LANES]
        for t in range(DEC_SEQ):
            qt = _rope(qc[t * nb:(t + 1) * nb, :], *tabs(t)) * (HEAD_DIM ** -0.5)
            q_ref[pl.ds((j // 2) * 8 + t * 2 + (j % 2), nb, stride=16), :] = qt
    kz = z[:, W_A + W_B:W_A + W_B + KV_W]
    vz = z[:, W_A + W_B + KV_W:W_A + W_B + 2 * KV_W]
    for t in range(DEC_SEQ):
        kn_ref[pl.ds(t, nb, stride=DEC_SEQ), :] = _rope(kz[t * nb:(t + 1) * nb, :], *tabs(t))
        vn_ref[pl.ds(t, nb, stride=DEC_SEQ), :] = vz[t * nb:(t + 1) * nb, :]
    u_ref[...] = z[:, W_A + W_B + 2 * KV_W:MIX_COLS]


def _proj_sample_call(l, x2d, mod, wts, rope_tabs, pool_prev):
    rows = DEC_BATCH * DEC_SEQ
    full = lambda shape: pl.BlockSpec(shape, lambda i: (0,) * len(shape))
    out_shape = [
        jax.ShapeDtypeStruct((rows, W_A), BF16),
        jax.ShapeDtypeStruct((DEC_BATCH * 16, LANES), F32),
        jax.ShapeDtypeStruct((rows, KV_W), F32),
        jax.ShapeDtypeStruct((rows, KV_W), F32),
        jax.ShapeDtypeStruct((rows, W_C), F32),
        jax.ShapeDtypeStruct((DEC_BATCH, POOL_BUF * W_A), F32),
    ]
    return pl.pallas_call(
        _proj_sample_kernel,
        grid=(1,),
        in_specs=[
            full(x2d.shape), _mod_spec(l, True, 0), _mod_spec(l, True, 1),
            _layer_spec(l, (1, D_MODEL)),
            _layer_spec(l, (D_MODEL, MIX_COLS)),
            _layer_spec(l, (2, 2 * POOL_GROUP, 2 * POOL_GROUP)),
            _layer_spec(l, (1, W_A)),
            full(rope_tabs[0].shape), full(rope_tabs[1].shape), full(rope_tabs[2].shape),
            _layer_spec(l, (DEC_BATCH, POOL_BUF * W_A)),
        ],
        out_specs=[full(s.shape) for s in out_shape],
        out_shape=out_shape,
        compiler_params=_cparams(("arbitrary",)),
        name="proj_sample",
    )(x2d, mod, mod, wts["g1"], wts["w_in"], wts["pool_w"], wts["pool_scale"], *rope_tabs, pool_prev)


def _attn_sample_kernel(q_ref, kc_ref, vc_ref, kn_ref, vn_ref, sink_ref, yb_ref, ko_ref, vo_ref):
    nn = 2 * SUBLANES
    nprob = KV_HEADS * ATT_SEQS
    nrow = nprob * SUBLANES
    lo_c = lax.broadcasted_iota(jnp.int32, (WINDOW, LANES), 1) < HEAD_DIM
    lo_n = lax.broadcasted_iota(jnp.int32, (SUBLANES, LANES), 1) < HEAD_DIM
    pad = jnp.zeros((SUBLANES - DEC_SEQ, LANES), F32)
    nt_dims = (((1,), (1,)), ((), ()))

    for i in range(ATT_SEQS):
        ko_ref[i, 0:WINDOW - DEC_SEQ, :] = kc_ref[i, DEC_SEQ:WINDOW, :]
        ko_ref[i, WINDOW - DEC_SEQ:WINDOW, :] = kn_ref[i * DEC_SEQ:(i + 1) * DEC_SEQ, :]
        vo_ref[i, 0:WINDOW - DEC_SEQ, :] = vc_ref[i, DEC_SEQ:WINDOW, :]
        vo_ref[i, WINDOW - DEC_SEQ:WINDOW, :] = vn_ref[i * DEC_SEQ:(i + 1) * DEC_SEQ, :]

    def pair_operands(cache, new):
        new8 = jnp.concatenate([new, pad], axis=0)
        cr, nr = pltpu.roll(cache, HEAD_DIM, axis=1), pltpu.roll(new8, HEAD_DIM, axis=1)
        out = []
        for g in range(KV_HEADS):
            (ca, cb), (na, nb) = ((cache, cr), (new8, nr)) if g == 0 else ((cr, cache), (nr, new8))
            out.append(jnp.concatenate([_head_pair_operand(lo_c, ca, cb), _head_pair_operand(lo_n, na, nb)],
                                       axis=0).astype(BF16))
        return out

    kks, vvs = [], []
    for i in range(ATT_SEQS):
        kks += pair_operands(kc_ref[i], kn_ref[i * DEC_SEQ:(i + 1) * DEC_SEQ, :])
        vvs += pair_operands(vc_ref[i], vn_ref[i * DEC_SEQ:(i + 1) * DEC_SEQ, :])

    s = jnp.concatenate(
        [lax.dot_general(q_ref[k * SUBLANES:(k + 1) * SUBLANES, :].astype(BF16), kks[k], nt_dims,
                         preferred_element_type=F32) for k in range(nprob)], axis=0)
    tq_c = jnp.right_shift(jnp.bitwise_and(lax.broadcasted_iota(jnp.int32, (nrow, WINDOW), 0), SUBLANES - 1), 1)
    mask_c = lax.broadcasted_iota(jnp.int32, (nrow, WINDOW), 1) > tq_c
    tq_n = jnp.right_shift(jnp.bitwise_and(lax.broadcasted_iota(jnp.int32, (nrow, nn), 0), SUBLANES - 1), 1)
    col_n = lax.broadcasted_iota(jnp.int32, (nrow, nn), 1)
    vis_n = jnp.bitwise_and(col_n, SUBLANES - 1) <= tq_n
    half_n = jnp.right_shift(col_n, 3)
    s_n = s[:, 2 * WINDOW:2 * WINDOW + nn]
    pcs = []
    pn = jnp.zeros((nrow, nn), F32)
    for hh in range(2):
        sk = sink_ref[:, hh * LANES:hh * LANES + 1]
        sc = jnp.where(mask_c, s[:, hh * WINDOW:(hh + 1) * WINDOW], NEG)
        sn = jnp.where(vis_n & (half_n == hh), s_n, NEG)
        m = jnp.maximum(jnp.maximum(jnp.max(sc, axis=-1, keepdims=True), jnp.max(sn, axis=-1, keepdims=True)), sk)
        pc = jnp.exp(sc - m)
        ph = jnp.exp(sn - m)
        den = jnp.sum(pc, axis=-1, keepdims=True) + jnp.sum(ph, axis=-1, keepdims=True) + jnp.exp(sk - m)
        pcs.append(pc / den)
        pn = pn + ph / den
    p = jnp.concatenate(pcs + [pn], axis=1)
    yb_ref[...] = jnp.concatenate(
        [_dot(p[k * SUBLANES:(k + 1) * SUBLANES, :].astype(BF16), vvs[k]) for k in range(nprob)], axis=0)


def _attn_sample_call(l, q, kc, vc, kn, vn, sink_rows):
    bs = ATT_SEQS
    cache_in = pl.BlockSpec((None, bs, WINDOW, KV_W), lambda i: (l, i, 0, 0))
    cache_out = pl.BlockSpec((bs, WINDOW, KV_W), lambda i: (i, 0, 0))
    new_spec = pl.BlockSpec((bs * DEC_SEQ, KV_W), lambda i: (i, 0))
    q_spec = pl.BlockSpec((bs * 16, LANES), lambda i: (i, 0))
    return pl.pallas_call(
        _attn_sample_kernel,
        grid=(DEC_BATCH // bs,),
        in_specs=[q_spec, cache_in, cache_in, new_spec, new_spec, _layer_spec(l, (bs * 16, 2 * LANES))],
        out_specs=[q_spec, cache_out, cache_out],
        out_shape=[
            jax.ShapeDtypeStruct((DEC_BATCH * 16, LANES), F32),
            jax.ShapeDtypeStruct((DEC_BATCH, WINDOW, KV_W), F32),
            jax.ShapeDtypeStruct((DEC_BATCH, WINDOW, KV_W), F32),
        ],
        compiler_params=_cparams(("arbitrary",)),
        name="attn_sample",
    )(q, kc, vc, kn, vn, sink_rows)


def _rope_tables(pos):
    inv = ROPE_THETA ** (-jnp.arange(0, ROT_DIM, 2, dtype=F32) / ROT_DIM)
    ang = pos.astype(F32)[:, None] * inv[None, :]
    cos, sin = jnp.cos(ang), jnp.sin(ang)
    n = pos.shape[0]
    half = ROT_DIM // 2
    rest = HEAD_DIM - ROT_DIM
    z8 = jnp.zeros((n, half), F32)
    zr = jnp.zeros((n, rest), F32)
    c64 = jnp.concatenate([cos, cos, jnp.ones((n, rest), F32)], axis=1)
    sa64 = jnp.concatenate([z8, sin, zr], axis=1)
    sb64 = jnp.concatenate([-sin, z8, zr], axis=1)
    return tuple(jnp.tile(t, (1, LANES // HEAD_DIM)) for t in (c64, sa64, sb64))


def _ssm_params(a_re, a_im, log_dt, b_re, b_im, c_re, c_im):
    a_re, a_im = a_re.astype(F32), a_im.astype(F32)
    dt = jnp.exp(log_dt.astype(F32))[..., None]
    mag = jnp.exp(a_re * dt)
    abar_re, abar_im = mag * jnp.cos(a_im * dt), mag * jnp.sin(a_im * dt)
    inv = 1.0 / (a_re * a_re + a_im * a_im)
    f_re = (((abar_re - 1.0) * a_re + abar_im * a_im) * inv)[..., None]
    f_im = ((abar_im * a_re - (abar_re - 1.0) * a_im) * inv)[..., None]
    b_re, b_im = b_re.astype(F32), b_im.astype(F32)
    bbar_re = f_re * b_re - f_im * b_im
    bbar_im = f_re * b_im + f_im * b_re
    half = N_STATE // 2
    gpt = SSM_GROUPS // 2

    def diag_tiles(m):
        rr, cc = m.shape[2], m.shape[3]
        rep = jnp.tile(m.reshape(DEPTH, 2, gpt * rr, cc), (1, 1, 1, gpt))
        same = (jnp.arange(gpt * rr)[:, None] // rr) == (jnp.arange(gpt * cc)[None, :] // cc)
        return jnp.where(same, rep, 0.0).astype(BF16)

    diag_in = lambda m: diag_tiles(jnp.swapaxes(m, 2, 3))
    diag_out = lambda m: diag_tiles(jnp.swapaxes(m, 2, 3))

    return dict(
        ssm_a_re=abar_re.reshape(DEPTH, 1, N_STATE), ssm_a_im=abar_im.reshape(DEPTH, 1, N_STATE),
        ssm_b_re=diag_in(bbar_re), ssm_b_im=diag_in(bbar_im),
        ssm_c_re=diag_out(c_re.astype(F32)), ssm_c_imn=diag_out(-c_im.astype(F32)),
    )


def _pool_weight_pairs(pool_w):
    z = jnp.zeros((DEPTH, POOL_GROUP, POOL_GROUP), pool_w.dtype)
    pair = lambda a, b: jnp.concatenate([jnp.concatenate([a, z], axis=2), jnp.concatenate([z, b], axis=2)], axis=1)
    return jnp.stack([pair(pool_w[:, 0], pool_w[:, 1]), pair(pool_w[:, 2], pool_w[:, 3])], axis=1).astype(BF16)


def _sink_rows_sample(sinks):
    pairs = jnp.repeat(sinks.astype(F32).reshape(DEPTH, N_HEADS // 2, 2), LANES, axis=2)
    pairs = pairs.reshape(DEPTH, 1, KV_HEADS, 1, 2, 2 * LANES)
    rows = jnp.broadcast_to(pairs, (DEPTH, ATT_SEQS, KV_HEADS, DEC_SEQ, 2, 2 * LANES))
    return rows.reshape(DEPTH, ATT_SEQS * 16, 2 * LANES)


def kernel(x_prompt, x_sample, cache_win_k, cache_win_v, state_pool, state_ssm_re, state_ssm_im, c_prompt, c_sample, norm1_g, norm2_g, w_ada, b_ada, w_in, pool_w, pool_scale, attn_sinks, ssm_a_re, ssm_a_im, ssm_log_dt, ssm_b_re, ssm_b_im, ssm_c_re, ssm_c_im, ssm_d, w_glu, w_branch_a, w_branch_b, w_branch_c, w_out, w_ffn_in, w_ffn_out, final_norm_g):
    mod = _ada_call(jnp.concatenate([c_sample, c_prompt], axis=0), w_ada, b_ada)
    tabs_p = _rope_tables(jnp.arange(SEQ, dtype=jnp.int32))
    tabs_s = _rope_tables(PAST_LEN + jnp.arange(DEC_SEQ, dtype=jnp.int32))
    g_final = final_norm_g.reshape(1, D_MODEL)

    wts = dict(
        g1=norm1_g.reshape(DEPTH, 1, D_MODEL), g2=norm2_g.reshape(DEPTH, 1, D_MODEL),
        w_in=w_in.astype(BF16),
        pool_w=_pool_weight_pairs(pool_w), pool_scale=pool_scale.reshape(DEPTH, 1, W_A),
        ssm_d=ssm_d.reshape(DEPTH, 1, W_C), w_glu=w_glu.astype(BF16),
        w_a=w_branch_a.astype(BF16), w_b=w_branch_b.astype(BF16), w_c=w_branch_c.astype(BF16),
        w_out=w_out.astype(BF16), w_ffn_in=w_ffn_in.astype(BF16), w_ffn_out=w_ffn_out.astype(BF16),
        **_ssm_params(ssm_a_re, ssm_a_im, ssm_log_dt, ssm_b_re, ssm_b_im, ssm_c_re, ssm_c_im),
    )
    sinks_p = attn_sinks.astype(F32)
    sinks_s = _sink_rows_sample(attn_sinks)
    cache_k = cache_win_k.reshape(DEPTH, DEC_BATCH, WINDOW, KV_W)
    cache_v = cache_win_v.reshape(DEPTH, DEC_BATCH, WINDOW, KV_W)
    pool_prev = state_pool.reshape(DEPTH, DEC_BATCH, POOL_BUF * W_A)
    h0_re = state_ssm_re.reshape(DEPTH, DEC_BATCH, N_STATE)
    h0_im = state_ssm_im.reshape(DEPTH, DEC_BATCH, N_STATE)
    zeros_state = jnp.zeros((1, BATCH, N_STATE), F32)

    hp = x_prompt
    hs = x_sample.reshape(DEC_BATCH, DEC_SEQ * D_MODEL)
    outs = [[] for _ in range(10)]
    for l in range(DEPTH):
        last = l == DEPTH - 1

        ya, q, kv, u, kwin, vwin, pool_new = _proj_prompt_call(l, hp, mod, wts, tabs_p)
        yb = _attn_prompt_call(l, q, kv, sinks_p)
        yc, hre, him = _ssm_call(l, u.reshape(BATCH, SEQ, W_C), zeros_state, zeros_state, 0, wts,
                                 BATCH, SSM_TC, True, "ssm_prompt")
        hp = _merge_call(l, False, hp, mod, ya, yb, yc.reshape(BATCH * SEQ, W_C), wts)
        hp = _ffn_call(l, False, last, hp, mod, wts, g_final)
        for k, v in zip(range(5), (kwin.reshape(BATCH, WINDOW, KV_HEADS, HEAD_DIM),
                                   vwin.reshape(BATCH, WINDOW, KV_HEADS, HEAD_DIM), pool_new,
                                   hre.reshape(BATCH, SSM_GROUPS, SSM_STATE),
                                   him.reshape(BATCH, SSM_GROUPS, SSM_STATE))):
            outs[k].append(v)

        ya, q, kn, vn, u, pool_new = _proj_sample_call(l, hs, mod, wts, tabs_s, pool_prev)
        yb, k_new, v_new = _attn_sample_call(l, q, cache_k, cache_v, kn, vn, sinks_s)
        yc, hre, him = _ssm_call(l, u, h0_re, h0_im, l, wts, DEC_BATCH, DEC_SEQ, False, "ssm_sample")
        hs = _merge_call(l, True, hs, mod, ya, yb, yc, wts)
        hs = _ffn_call(l, True, last, hs, mod, wts, g_final)
        for k, v in zip(range(5, 10), (k_new.reshape(DEC_BATCH, WINDOW, KV_HEADS, HEAD_DIM),
                                       v_new.reshape(DEC_BATCH, WINDOW, KV_HEADS, HEAD_DIM),
                                       pool_new.reshape(DEC_BATCH, POOL_BUF, W_A),
                                       hre.reshape(DEC_BATCH, SSM_GROUPS, SSM_STATE),
                                       him.reshape(DEC_BATCH, SSM_GROUPS, SSM_STATE))):
            outs[k].append(v)

    return (hp, hs.reshape(DEC_BATCH, DEC_SEQ, D_MODEL), *[jnp.stack(o) for o in outs])
```

```python
import functools

import jax
import jax.numpy as jnp
from jax import lax
from jax.experimental import pallas as pl
from jax.experimental.pallas import tpu as pltpu

F32 = jnp.float32
BF16 = jnp.bfloat16

D_MODEL = 1024
BATCH = 8
SEQ = 2048
DEPTH = 4
DEC_BATCH = 128
DEC_SEQ = 4
PAST_LEN = 8192

W_A = 512
POOL_WINDOWS = (2, 4, 8, 16)
POOL_GROUP = 128
POOL_BUF = 15
POOL_HIST = 32
HEAD_DIM = 64
N_HEADS = 8
KV_HEADS = 2
W_B = 512
KV_W = 128
WINDOW = 128
ROT_DIM = 16
ROPE_THETA = 500000.0
W_C = 512
SSM_CH = 16
SSM_GROUPS = 32
SSM_STATE = 64
N_STATE = SSM_GROUPS * SSM_STATE
D_FF = 2816
EPS = 1e-6
MIX_COLS = W_A + W_B + 2 * KV_W + W_C

LANES = 128
SUBLANES = 8
VMEM_LIMIT = 56 * 1024 * 1024

TM = 1024
NT = SEQ // TM
PROJ_SUB = 128
ATT_BLK = WINDOW
ATT_SUB = 8
TM_WIDE = 1024
ATT_SEQS = 16
SSM_TC = 128
SSM_SUB = 128
SCAN_CW = 512
NEG = -1e30
FF_CHUNKS = ((0, 1024), (1024, 1024), (2048, 768))
PROMPT_MOD_BLOCK = DEC_BATCH // BATCH


def _cparams(sem):
    return pltpu.CompilerParams(dimension_semantics=sem, vmem_limit_bytes=VMEM_LIMIT)


def _dot(a, b):
    return jnp.dot(a, b, preferred_element_type=F32)


def _norm_mod(x, g, sc, sh):
    r = lax.rsqrt(jnp.mean(x * x, axis=-1, keepdims=True) + EPS)
    return (x * r) * g * (1.0 + sc) + sh


def _mod_rows(ref, sample):
    return ref[...] if sample else ref[pl.ds(pl.program_id(0), 1), :]


def _rope(x, cos, sa, sb):
    return x * cos + pltpu.roll(x, 8, axis=1) * sa + pltpu.roll(x, LANES - 8, axis=1) * sb


def _layer_spec(l, tail, **kw):
    nz = (0,) * len(tail)
    return pl.BlockSpec((None, *tail), lambda *_: (l, *nz), **kw)


def _mod_spec(l, sample, j):
    if sample:
        return pl.BlockSpec((None, DEC_BATCH, D_MODEL), lambda *_: (l, 0, j))
    return pl.BlockSpec((None, BATCH, D_MODEL), lambda *_: (l, PROMPT_MOD_BLOCK, j))


def _ada_kernel(c_ref, w_ref, b_ref, o_ref):
    c = c_ref[...]
    a = (c * jax.nn.sigmoid(c)).astype(BF16)
    o_ref[...] = _dot(a, w_ref[...].astype(BF16)) + b_ref[...]


def _ada_call(c_all, w_ada, b_ada):
    n = c_all.shape[0]
    cb = 1536
    return pl.pallas_call(
        _ada_kernel,
        grid=(DEPTH, 6 * D_MODEL // cb),
        in_specs=[
            pl.BlockSpec((n, D_MODEL), lambda l, j: (0, 0)),
            pl.BlockSpec((None, D_MODEL, cb), lambda l, j: (l, 0, j)),
            pl.BlockSpec((None, 1, cb), lambda l, j: (l, 0, j)),
        ],
        out_specs=pl.BlockSpec((None, n, cb), lambda l, j: (l, 0, j)),
        out_shape=jax.ShapeDtypeStruct((DEPTH, n, 6 * D_MODEL), F32),
        compiler_params=_cparams(("arbitrary", "arbitrary")),
        name="ada",
    )(c_all, w_ada, b_ada.reshape(DEPTH, 1, 6 * D_MODEL))


def _proj_prompt_kernel(x_ref, sh_ref, sc_ref, g_ref, w_ref, pw_ref, ps_ref, cos_ref, sa_ref, sb_ref,
                        ya_ref, q_ref, kv_ref, u_ref, kwin_ref, vwin_ref, pool_ref, xe_ref, s2_ref, s4_ref, s8_ref):
    t = pl.program_id(1)
    tm = x_ref.shape[0]
    hist = POOL_HIST
    last = t == pl.num_programs(1) - 1
    sh = _mod_rows(sh_ref, False)
    sc = _mod_rows(sc_ref, False)
    g1, g2 = POOL_GROUP, 2 * POOL_GROUP

    @pl.when(t == 0)
    def _():
        xe_ref[0:hist, :] = jnp.zeros((hist, W_A), F32)

    for r0 in range(0, tm, PROJ_SUB):
        rows = slice(r0, r0 + PROJ_SUB)
        h = _norm_mod(x_ref[rows, :], g_ref[...], sc, sh).astype(BF16)
        z = _dot(h, w_ref[...])
        xa = z[:, 0:W_A]

        a = hist + r0
        e = a + PROJ_SUB
        xe_ref[a:e, :] = xa
        s2_ref[a - 24:e, :] = xe_ref[a - 24:e, :] + xe_ref[a - 25:e - 1, :]
        s4_ref[a - 16:e, :] = s2_ref[a - 16:e, g1:] + s2_ref[a - 18:e - 2, g1:]
        s8_ref[a - 8:e, :] = s4_ref[a - 8:e, g1:] + s4_ref[a - 12:e - 4, g1:]
        sums = (s2_ref[a:e, 0:g1], s4_ref[a:e, 0:g1], s8_ref[a:e, 0:g1],
                s8_ref[a:e, g1:g2] + s8_ref[a - 8:e - 8, g1:g2])
        pos = t * tm + r0 + lax.broadcasted_iota(jnp.int32, (PROJ_SUB, 1), 0)
        ds = []
        for g, w in enumerate(POOL_WINDOWS):
            cnt = jnp.minimum(pos + 1, w).astype(F32)
            ds.append((sums[g] / cnt - xa[:, g * POOL_GROUP:(g + 1) * POOL_GROUP]).astype(BF16))
        for k in range(2):
            y = _dot(jnp.concatenate(ds[2 * k:2 * k + 2], axis=1), pw_ref[k]) * ps_ref[:, k * g2:(k + 1) * g2]
            ya_ref[rows, k * g2:(k + 1) * g2] = y.astype(BF16)

        cos = cos_ref[rows, :]
        sa = sa_ref[rows, :]
        sb = sb_ref[rows, :]
        for c in range(W_B // LANES):
            qc = z[:, W_A + c * LANES:W_A + (c + 1) * LANES]
            q_ref[rows, c * LANES:(c + 1) * LANES] = (_rope(qc, cos, sa, sb) * (HEAD_DIM ** -0.5)).astype(BF16)
        kz = _rope(z[:, W_A + W_B:W_A + W_B + KV_W], cos, sa, sb)
        vz = z[:, W_A + W_B + KV_W:W_A + W_B + 2 * KV_W]
        kv_ref[rows, 0:128] = kz.astype(BF16)
        kv_ref[rows, 128:256] = pltpu.roll(kz, HEAD_DIM, axis=1).astype(BF16)
        kv_ref[rows, 256:384] = vz.astype(BF16)
        kv_ref[rows, 384:512] = pltpu.roll(vz, HEAD_DIM, axis=1).astype(BF16)
        if r0 + PROJ_SUB == tm:
            @pl.when(last)
            def _(kz=kz, vz=vz):
                kwin_ref[...] = kz[PROJ_SUB - WINDOW:, :]
                vwin_ref[...] = vz[PROJ_SUB - WINDOW:, :]

        u_ref[rows, :] = z[:, W_A + W_B + 2 * KV_W:MIX_COLS]

    @pl.when(last)
    def _():
        pool_ref[...] = xe_ref[hist + tm - POOL_BUF:hist + tm, :]

    xe_ref[0:hist, :] = xe_ref[tm:tm + hist, :]


def _proj_prompt_call(l, x, mod, wts, rope_tabs):
    rows = BATCH * SEQ
    row_blk = lambda b, t: (b * NT + t, 0)
    tab_spec = pl.BlockSpec((TM, LANES), lambda b, t: (t, 0))
    return pl.pallas_call(
        _proj_prompt_kernel,
        grid=(BATCH, NT),
        in_specs=[
            pl.BlockSpec((None, TM, D_MODEL), lambda b, t: (b, t, 0)),
            _mod_spec(l, False, 0), _mod_spec(l, False, 1),
            _layer_spec(l, (1, D_MODEL)),
            _layer_spec(l, (D_MODEL, MIX_COLS)),
            _layer_spec(l, (2, 2 * POOL_GROUP, 2 * POOL_GROUP)),
            _layer_spec(l, (1, W_A)),
            tab_spec, tab_spec, tab_spec,
        ],
        out_specs=[
            pl.BlockSpec((TM, W_A), row_blk),
            pl.BlockSpec((TM, W_B), row_blk),
            pl.BlockSpec((TM, 4 * KV_W), row_blk),
            pl.BlockSpec((TM, W_C), row_blk),
            pl.BlockSpec((None, WINDOW, KV_W), lambda b, t: (b, 0, 0)),
            pl.BlockSpec((None, WINDOW, KV_W), lambda b, t: (b, 0, 0)),
            pl.BlockSpec((None, POOL_BUF, W_A), lambda b, t: (b, 0, 0)),
        ],
        out_shape=[
            jax.ShapeDtypeStruct((rows, W_A), BF16),
            jax.ShapeDtypeStruct((rows, W_B), BF16),
            jax.ShapeDtypeStruct((rows, 4 * KV_W), BF16),
            jax.ShapeDtypeStruct((rows, W_C), F32),
            jax.ShapeDtypeStruct((BATCH, WINDOW, KV_W), F32),
            jax.ShapeDtypeStruct((BATCH, WINDOW, KV_W), F32),
            jax.ShapeDtypeStruct((BATCH, POOL_BUF, W_A), F32),
        ],
        scratch_shapes=[pltpu.VMEM((TM + POOL_HIST, W_A), F32),
                        pltpu.VMEM((TM + POOL_HIST, W_A), F32),
                        pltpu.VMEM((TM + POOL_HIST, W_A - POOL_GROUP), F32),
                        pltpu.VMEM((TM + POOL_HIST, W_A - 2 * POOL_GROUP), F32)],
        compiler_params=_cparams(("arbitrary", "arbitrary")),
        name="proj_prompt",
    )(x, mod, mod, wts["g1"], wts["w_in"], wts["pool_w"], wts["pool_scale"], *rope_tabs)


def _head_pair_operand(lo_mask, a, b):
    zero = jnp.zeros_like(a)
    return jnp.concatenate([jnp.where(lo_mask, a, zero), jnp.where(lo_mask, zero, b)], axis=0)


def _attn_window_block(l, q_ref, yb_ref, sink_ref, r0, kv_prev, kv_cur, prev_bias):
    n = ATT_BLK
    lo = lax.broadcasted_iota(jnp.int32, (n, LANES), 1) < HEAD_DIM
    rows = jnp.bitwise_and(lax.broadcasted_iota(jnp.int32, (2 * n, LANES), 0), n - 1)
    from_prev = lax.broadcasted_iota(jnp.int32, (2 * n, LANES), 1) > rows
    lo2 = lax.broadcasted_iota(jnp.int32, (2 * n, LANES), 1) < HEAD_DIM
    top = lax.broadcasted_iota(jnp.int32, (2 * n, 1), 0) < n
    zero = jnp.zeros((n, LANES), BF16)
    ones_a = jnp.where(lo, 1.0, 0.0).astype(BF16)
    ones_b = jnp.where(lo, 0.0, 1.0).astype(BF16)
    nt_dims = (((1,), (1,)), ((), ()))
    for g in range(KV_HEADS):
        ca, cb = (0, LANES) if g == 0 else (LANES, 0)
        kk = jnp.concatenate([jnp.where(lo, kv_prev[:, ca:ca + LANES], zero),
                              jnp.where(lo, kv_cur[:, ca:ca + LANES], zero),
                              jnp.where(lo, zero, kv_prev[:, cb:cb + LANES]),
                              jnp.where(lo, zero, kv_cur[:, cb:cb + LANES])], axis=0)
        va, vb = 2 * LANES + ca, 2 * LANES + cb
        vv = jnp.concatenate([
            jnp.concatenate([jnp.where(lo, kv_prev[:, va:va + LANES], zero), ones_a], axis=1),
            jnp.concatenate([jnp.where(lo, kv_cur[:, va:va + LANES], zero), ones_a], axis=1),
            jnp.concatenate([jnp.where(lo, zero, kv_prev[:, vb:vb + LANES]), ones_b], axis=1),
            jnp.concatenate([jnp.where(lo, zero, kv_cur[:, vb:vb + LANES]), ones_b], axis=1)], axis=0)
        qs = jnp.concatenate([q_ref[pl.ds(r0, n), (2 * g) * LANES:(2 * g + 1) * LANES],
                              q_ref[pl.ds(r0, n), (2 * g + 1) * LANES:(2 * g + 2) * LANES]], axis=0)
        s = lax.dot_general(qs, kk, nt_dims, preferred_element_type=F32)
        ps, es = [], []
        for hh in range(2):
            s_prev = s[:, hh * 2 * n:hh * 2 * n + n]
            if prev_bias is not None:
                s_prev = s_prev + prev_bias
            logits = jnp.where(from_prev, s_prev, s[:, hh * 2 * n + n:(hh + 1) * 2 * n])
            sk = jnp.where(top, sink_ref[l, 4 * g + hh], sink_ref[l, 4 * g + 2 + hh])
            m = jnp.maximum(jnp.max(logits, axis=-1, keepdims=True), sk)
            p = jnp.exp(logits - m)
            ps += [jnp.where(from_prev, p, 0.0), jnp.where(from_prev, 0.0, p)]
            es.append(jnp.exp(sk - m))
        r = _dot(jnp.concatenate(ps, axis=1).astype(BF16), vv)
        o = r[:, 0:LANES] / (r[:, LANES:2 * LANES] + jnp.where(lo2, es[0], es[1]))
        yb_ref[pl.ds(r0, n), (2 * g) * LANES:(2 * g + 1) * LANES] = o[0:n].astype(BF16)
        yb_ref[pl.ds(r0, n), (2 * g + 1) * LANES:(2 * g + 2) * LANES] = o[n:2 * n].astype(BF16)


def _attn_prompt_kernel(l, sink_ref, q_ref, kvc_ref, kvp_ref, yb_ref):
    i = pl.program_id(1)
    first_bias = jnp.where(i > 0, 0.0, NEG).astype(F32)
    for sub in range(ATT_SUB):
        r0 = sub * ATT_BLK
        kv_cur = kvc_ref[r0:r0 + ATT_BLK, :]
        kv_prev = kvp_ref[...] if sub == 0 else kvc_ref[r0 - ATT_BLK:r0, :]
        _attn_window_block(l, q_ref, yb_ref, sink_ref, r0, kv_prev, kv_cur, first_bias if sub == 0 else None)


def _attn_prompt_call(l, q, kv, sinks):
    rows = ATT_SUB * ATT_BLK
    nb = SEQ // rows
    return pl.pallas_call(
        functools.partial(_attn_prompt_kernel, l),
        grid=(BATCH, nb),
        in_specs=[
            pl.BlockSpec(memory_space=pltpu.SMEM),
            pl.BlockSpec((rows, W_B), lambda b, i: (b * nb + i, 0)),
            pl.BlockSpec((rows, 4 * KV_W), lambda b, i: (b * nb + i, 0)),
            pl.BlockSpec((ATT_BLK, 4 * KV_W), lambda b, i: ((b * nb + i) * ATT_SUB - jnp.minimum(i, 1), 0)),
        ],
        out_specs=pl.BlockSpec((rows, W_B), lambda b, i: (b * nb + i, 0)),
        out_shape=jax.ShapeDtypeStruct((BATCH * SEQ, W_B), BF16),
        compiler_params=_cparams(("arbitrary", "arbitrary")),
        name="attn_prompt",
    )(sinks, q, kv, kv)


def _ssm_kernel(rows_per_step, seq_major, u_ref, h0re_ref, h0im_ref, are_ref, aim_ref, bre_ref, bim_ref, cre_ref,
                cimn_ref, d_ref, wglu_ref, yc_ref, hre_out, him_out, dre, dim_, hre, him, *slabs):
    i = pl.program_id(0)
    r = rows_per_step
    half = N_STATE // 2
    nslab = W_C // LANES
    blk = dre.shape[0]
    steps = SSM_SUB // r

    @pl.when(i == 0)
    def _():
        hre[...] = h0re_ref[...]
        him[...] = h0im_ref[...]

    if seq_major:
        (perm,) = slabs
        tc = u_ref.shape[1]
        for b in range(r):
            for c in range(nslab):
                perm[c, pl.ds(b, tc, stride=r), :] = u_ref[b, :, c * LANES:(c + 1) * LANES]

    def u_rows(rows):
        if seq_major:
            return jnp.concatenate([perm[c, rows, :] for c in range(nslab)], axis=1)
        return u_ref[rows, :]

    def drive(s):
        rows = slice(s * SSM_SUB, (s + 1) * SSM_SUB)
        ub = u_rows(rows).astype(BF16)
        for kt in range(2):
            uk = ub[:, kt * 256:(kt + 1) * 256]
            dre[rows, kt * half:(kt + 1) * half] = _dot(uk, bre_ref[kt])
            dim_[rows, kt * half:(kt + 1) * half] = _dot(uk, bim_ref[kt])

    def scan(s):
        for rt in range(r // SUBLANES):
            rsl = slice(rt * SUBLANES, (rt + 1) * SUBLANES)
            for c in range(N_STATE // SCAN_CW):
                cols = slice(c * SCAN_CW, (c + 1) * SCAN_CW)
                ar = jnp.broadcast_to(are_ref[:, cols], (SUBLANES, SCAN_CW))
                ai = jnp.broadcast_to(aim_ref[:, cols], (SUBLANES, SCAN_CW))
                pr, pi = hre[rsl, cols], him[rsl, cols]
                for t in range(steps):
                    row = s * SSM_SUB + t * r + rt * SUBLANES
                    trow = slice(row, row + SUBLANES)
                    nr = ar * pr - ai * pi + dre[trow, cols]
                    ni = ar * pi + ai * pr + dim_[trow, cols]
                    dre[trow, cols] = nr
                    dim_[trow, cols] = ni
                    pr, pi = nr, ni
                hre[rsl, cols] = pr
                him[rsl, cols] = pi

    def project(s):
        rows = slice(s * SSM_SUB, (s + 1) * SSM_SUB)
        parts = []
        for nt in range(2):
            sl = slice(nt * half, (nt + 1) * half)
            parts.append(_dot(dre[rows, sl].astype(BF16), cre_ref[nt]) + _dot(dim_[rows, sl].astype(BF16), cimn_ref[nt]))
        return jnp.concatenate(parts, axis=1)

    nsub = blk // SSM_SUB
    drive(0)
    ys = []
    for s in range(nsub):
        if s + 1 < nsub:
            drive(s + 1)
        scan(s)
        ys.append(project(s))

    @pl.when(i == pl.num_programs(0) - 1)
    def _():
        hre_out[...] = hre[...]
        him_out[...] = him[...]

    y = jnp.concatenate(ys, axis=0)
    if seq_major:
        for c in range(nslab):
            perm[c] = y[:, c * LANES:(c + 1) * LANES]
        y = jnp.concatenate(
            [jnp.concatenate([perm[c, pl.ds(b, tc, stride=r), :] for c in range(nslab)], axis=1) for b in range(r)],
            axis=0)
        u = u_ref[...].reshape(r * tc, W_C)
    else:
        u = u_ref[...]
    y = jax.nn.gelu(y + d_ref[...] * u)
    yc = (y * jax.nn.sigmoid(_dot(y.astype(BF16), wglu_ref[...]))).astype(BF16)
    yc_ref[...] = yc.reshape(yc_ref.shape)


def _ssm_call(l, u, h0re, h0im, h0_layer, wts, rows_per_step, steps_per_block, seq_major, name):
    blk = rows_per_step * steps_per_block
    half = N_STATE // 2
    if seq_major:
        nsteps = u.shape[1] // steps_per_block
        io_spec = pl.BlockSpec((rows_per_step, steps_per_block, W_C), lambda i: (0, i, 0))
        scratch = [pltpu.VMEM((W_C // LANES, blk, LANES), F32)]
    else:
        nsteps = u.shape[0] // blk
        io_spec = pl.BlockSpec((blk, W_C), lambda i: (i, 0))
        scratch = []
    state_spec = pl.BlockSpec((rows_per_step, N_STATE), lambda i: (0, 0))
    h0_spec = _layer_spec(h0_layer, (rows_per_step, N_STATE))
    return pl.pallas_call(
        functools.partial(_ssm_kernel, rows_per_step, seq_major),
        grid=(nsteps,),
        in_specs=[
            io_spec, h0_spec, h0_spec,
            _layer_spec(l, (1, N_STATE)), _layer_spec(l, (1, N_STATE)),
            _layer_spec(l, (2, 256, half)), _layer_spec(l, (2, 256, half)),
            _layer_spec(l, (2, half, 256)), _layer_spec(l, (2, half, 256)),
            _layer_spec(l, (1, W_C)),
            _layer_spec(l, (W_C, W_C)),
        ],
        out_specs=[io_spec, state_spec, state_spec],
        out_shape=[
            jax.ShapeDtypeStruct(u.shape, BF16),
            jax.ShapeDtypeStruct((rows_per_step, N_STATE), F32),
            jax.ShapeDtypeStruct((rows_per_step, N_STATE), F32),
        ],
        scratch_shapes=[
            pltpu.VMEM((blk, N_STATE), F32),
            pltpu.VMEM((blk, N_STATE), F32),
            pltpu.VMEM((rows_per_step, N_STATE), F32),
            pltpu.VMEM((rows_per_step, N_STATE), F32),
        ] + scratch,
        compiler_params=_cparams(("arbitrary",)),
        name=name,
    )(u, h0re, h0im, wts["ssm_a_re"], wts["ssm_a_im"], wts["ssm_b_re"], wts["ssm_b_im"], wts["ssm_c_re"],
      wts["ssm_c_imn"], wts["ssm_d"], wts["w_glu"])


def _merge_kernel(sample, x_ref, sh_ref, sc_ref, gt_ref, g_ref, ya_ref, yb_ref, yc_ref,
                  wg_ref, wa_ref, wb_ref, wc_ref, wo_ref, o_ref):
    x = x_ref[...]
    h = _norm_mod(x, g_ref[...], _mod_rows(sc_ref, sample), _mod_rows(sh_ref, sample)).astype(BF16)
    if sample:
        t = pl.program_id(1)
        parts = []
        for j in range(N_HEADS // 2):
            start = (j // 2) * 8 + t * 2 + (j % 2)
            parts.append(yb_ref[pl.ds(start, DEC_BATCH, stride=16), :])
        yb = jnp.concatenate(parts, axis=1).astype(BF16)
    else:
        yb = yb_ref[...]
    merged = None
    for k, (y, w_ref) in enumerate(((ya_ref[...], wa_ref), (yb, wb_ref), (yc_ref[...], wc_ref))):
        gate = jax.nn.sigmoid(_dot(h, wg_ref[0, :, k * D_MODEL:(k + 1) * D_MODEL]))
        term = gate * _dot(y, w_ref[...])
        merged = term if merged is None else merged + term
    o_ref[...] = x + _mod_rows(gt_ref, sample) * _dot(merged.astype(BF16), wo_ref[...])


def _row_specs(sample):
    if sample:
        tm = DEC_BATCH
        grid = (1, DEC_SEQ)
        x_spec = pl.BlockSpec((tm, D_MODEL), lambda b, t: (0, t))
        row_spec = lambda cols: pl.BlockSpec((tm, cols), lambda b, t: (t, 0))
    else:
        tm = TM_WIDE
        nt = SEQ // tm
        grid = (BATCH, nt)
        x_spec = pl.BlockSpec((None, tm, D_MODEL), lambda b, t: (b, t, 0))
        row_spec = lambda cols: pl.BlockSpec((tm, cols), lambda b, t: (b * nt + t, 0))
    return grid, x_spec, row_spec


def _merge_call(l, sample, x, mod, ya, yb, yc, wts):
    grid, x_spec, row_spec = _row_specs(sample)
    yb_spec = pl.BlockSpec(yb.shape, lambda b, t: (0, 0)) if sample else row_spec(W_B)
    single = pl.Buffered(1)
    gate_spec = pl.BlockSpec((pl.Element(1), pl.Element(D_MODEL), pl.Element(3 * D_MODEL)),
                             lambda *_: (l, 0, MIX_COLS), pipeline_mode=single)
    return pl.pallas_call(
        functools.partial(_merge_kernel, sample),
        grid=grid,
        in_specs=[
            x_spec, _mod_spec(l, sample, 0), _mod_spec(l, sample, 1), _mod_spec(l, sample, 2),
            _layer_spec(l, (1, D_MODEL)),
            row_spec(W_A), yb_spec, row_spec(W_C),
            gate_spec,
            _layer_spec(l, (W_A, D_MODEL), pipeline_mode=single),
            _layer_spec(l, (W_B, D_MODEL), pipeline_mode=single),
            _layer_spec(l, (W_C, D_MODEL), pipeline_mode=single),
            _layer_spec(l, (D_MODEL, D_MODEL), pipeline_mode=single),
        ],
        out_specs=x_spec,
        out_shape=jax.ShapeDtypeStruct(x.shape, F32),
        compiler_params=_cparams(("arbitrary", "arbitrary")),
        name="merge_sample" if sample else "merge_prompt",
    )(x, mod, mod, mod, wts["g1"], ya, yb, yc, wts["w_in"], wts["w_a"], wts["w_b"], wts["w_c"], wts["w_out"])


def _ffn_kernel(sample, final, x_ref, sh_ref, sc_ref, gt_ref, g_ref, wi_ref, wo_ref, gf_ref, o_ref):
    x = x_ref[...]
    h = _norm_mod(x, g_ref[...], _mod_rows(sc_ref, sample), _mod_rows(sh_ref, sample)).astype(BF16)
    acc = None
    for lo, n in FF_CHUNKS:
        a = _dot(h, wi_ref[:, lo:lo + n])
        b = _dot(h, wi_ref[:, D_FF + lo:D_FF + lo + n])
        act = ((a * jax.nn.sigmoid(a)) * b).astype(BF16)
        part = _dot(act, wo_ref[lo:lo + n, :])
        acc = part if acc is None else acc + part
    y = x + _mod_rows(gt_ref, sample) * acc
    if final:
        r = lax.rsqrt(jnp.mean(y * y, axis=-1, keepdims=True) + EPS)
        y = (y * r) * gf_ref[...]
    o_ref[...] = y


def _ffn_call(l, sample, final, x, mod, wts, g_final):
    grid, x_spec, _ = _row_specs(sample)
    single = pl.Buffered(1)
    return pl.pallas_call(
        functools.partial(_ffn_kernel, sample, final),
        grid=grid,
        in_specs=[
            x_spec, _mod_spec(l, sample, 3), _mod_spec(l, sample, 4), _mod_spec(l, sample, 5),
            _layer_spec(l, (1, D_MODEL)),
            _layer_spec(l, (D_MODEL, 2 * D_FF), pipeline_mode=single),
            _layer_spec(l, (D_FF, D_MODEL), pipeline_mode=single),
            pl.BlockSpec((1, D_MODEL), lambda b, t: (0, 0)),
        ],
        out_specs=x_spec,
        out_shape=jax.ShapeDtypeStruct(x.shape, F32),
        compiler_params=_cparams(("arbitrary", "arbitrary")),
        name="ffn_sample" if sample else "ffn_prompt",
    )(x, mod, mod, mod, wts["g2"], wts["w_ffn_in"], wts["w_ffn_out"], g_final)


def _proj_sample_kernel(x_ref, sh_ref, sc_ref, g_ref, w_ref, pw_ref, ps_ref, cos_ref, sa_ref, sb_ref, pool_prev_ref,
                        ya_ref, q_ref, kn_ref, vn_ref, u_ref, pool_ref):
    nb = DEC_BATCH
    sh = sh_ref[...]
    sc = sc_ref[...]
    h = jnp.concatenate(
        [_norm_mod(x_ref[:, t * D_MODEL:(t + 1) * D_MODEL], g_ref[...], sc, sh) for t in range(DEC_SEQ)],
        axis=0).astype(BF16)

    z = _dot(h, w_ref[...])
    xa = z[:, 0:W_A]
    xe = [pool_prev_ref[j] for j in range(POOL_BUF)]
    xe += [xa[t * nb:(t + 1) * nb, :] for t in range(DEC_SEQ)]
    for j in range(POOL_BUF):
        pool_ref[j] = xe[DEC_SEQ + j]
    dgs = []
    for g, w in enumerate(POOL_WINDOWS):
        lo = g * POOL_GROUP
        ds = []
        for t in range(DEC_SEQ):
            s = xe[POOL_BUF + t][:, lo:lo + POOL_GROUP]
            for j in range(1, w):
                s = s + xe[POOL_BUF + t - j][:, lo:lo + POOL_GROUP]
            ds.append(s / float(w) - xe[POOL_BUF + t][:, lo:lo + POOL_GROUP])
        dgs.append(jnp.concatenate(ds, axis=0).astype(BF16))
    for k in range(2):
        cols = slice(2 * k * POOL_GROUP, 2 * (k + 1) * POOL_GROUP)
        y = _dot(jnp.concatenate(dgs[2 * k:2 * k + 2], axis=1), pw_ref[k]) * ps_ref[:, cols]
        ya_ref[:, cols] = y.astype(BF16)

    def tabs(t):
        return cos_ref[t:t + 1, :], sa_ref[t:t + 1, :], sb_ref[t:t + 1, :]

    for j in range(W_B // LANES):
        qc = z[:, W_A + j * LANES:W_A + (j + 1) * LANES]
        for t in range(DEC_SEQ):
            qt = _rope(qc[t * nb:(t + 1) * nb, :], *tabs(t)) * (HEAD_DIM ** -0.5)
            q_ref[pl.ds((j // 2) * 8 + t * 2 + (j % 2), nb, stride=16), :] = qt
    kz = z[:, W_A + W_B:W_A + W_B + KV_W]
    vz = z[:, W_A + W_B + KV_W:W_A + W_B + 2 * KV_W]
    for t in range(DEC_SEQ):
        kn_ref[pl.ds(t, nb, stride=DEC_SEQ), :] = _rope(kz[t * nb:(t + 1) * nb, :], *tabs(t))
        vn_ref[pl.ds(t, nb, stride=DEC_SEQ), :] = vz[t * nb:(t + 1) * nb, :]
    u_ref[...] = z[:, W_A + W_B + 2 * KV_W:MIX_COLS]


def _proj_sample_call(l, x2d, mod, wts, rope_tabs, pool_prev):
    rows = DEC_BATCH * DEC_SEQ
    full = lambda shape: pl.BlockSpec(shape, lambda i: (0,) * len(shape))
    out_shape = [
        jax.ShapeDtypeStruct((rows, W_A), BF16),
        jax.ShapeDtypeStruct((DEC_BATCH * 16, LANES), F32),
        jax.ShapeDtypeStruct((rows, KV_W), F32),
        jax.ShapeDtypeStruct((rows, KV_W), F32),
        jax.ShapeDtypeStruct((rows, W_C), F32),
        jax.ShapeDtypeStruct((POOL_BUF, DEC_BATCH, W_A), F32),
    ]
    return pl.pallas_call(
        _proj_sample_kernel,
        grid=(1,),
        in_specs=[
            full(x2d.shape), _mod_spec(l, True, 0), _mod_spec(l, True, 1),
            _layer_spec(l, (1, D_MODEL)),
            _layer_spec(l, (D_MODEL, MIX_COLS)),
            _layer_spec(l, (2, 2 * POOL_GROUP, 2 * POOL_GROUP)),
            _layer_spec(l, (1, W_A)),
            full(rope_tabs[0].shape), full(rope_tabs[1].shape), full(rope_tabs[2].shape),
            _layer_spec(l, (POOL_BUF, DEC_BATCH, W_A)),
        ],
        out_specs=[full(s.shape) for s in out_shape],
        out_shape=out_shape,
        compiler_params=_cparams(("arbitrary",)),
        name="proj_sample",
    )(x2d, mod, mod, wts["g1"], wts["w_in"], wts["pool_w"], wts["pool_scale"], *rope_tabs, pool_prev)


def _attn_sample_kernel(q_ref, kc_ref, vc_ref, kn_ref, vn_ref, sink_ref, yb_ref):
    nn = 2 * SUBLANES
    nprob = KV_HEADS * ATT_SEQS
    nrow = nprob * SUBLANES
    lo_n = lax.broadcasted_iota(jnp.int32, (SUBLANES, LANES), 1) < HEAD_DIM
    pad = jnp.zeros((SUBLANES - DEC_SEQ, LANES), F32)
    zero = jnp.zeros((HEAD_DIM, WINDOW), BF16)
    nt_dims = (((1,), (1,)), ((), ()))

    def cache_operand(ref, i, g):
        m = ref[i, g].astype(BF16)
        return jnp.concatenate([jnp.concatenate([m, zero], axis=1), jnp.concatenate([zero, m], axis=1)], axis=0)

    def new_operands(new):
        new8 = jnp.concatenate([new, pad], axis=0)
        nr = pltpu.roll(new8, HEAD_DIM, axis=1)
        return [_head_pair_operand(lo_n, *((new8, nr) if g == 0 else (nr, new8))).astype(BF16)
                for g in range(KV_HEADS)]

    kcs = [cache_operand(kc_ref, i, g) for i in range(ATT_SEQS) for g in range(KV_HEADS)]
    vcs = [cache_operand(vc_ref, i, g) for i in range(ATT_SEQS) for g in range(KV_HEADS)]
    kns, vns = [], []
    for i in range(ATT_SEQS):
        kns += new_operands(kn_ref[i * DEC_SEQ:(i + 1) * DEC_SEQ, :])
        vns += new_operands(vn_ref[i * DEC_SEQ:(i + 1) * DEC_SEQ, :])

    qs = [q_ref[k * SUBLANES:(k + 1) * SUBLANES, :].astype(BF16) for k in range(nprob)]
    s = jnp.concatenate(
        [jnp.concatenate([_dot(qs[k], kcs[k]),
                          lax.dot_general(qs[k], kns[k], nt_dims, preferred_element_type=F32)], axis=1)
         for k in range(nprob)], axis=0)
    tq_c = jnp.right_shift(jnp.bitwise_and(lax.broadcasted_iota(jnp.int32, (nrow, WINDOW), 0), SUBLANES - 1), 1)
    mask_c = lax.broadcasted_iota(jnp.int32, (nrow, WINDOW), 1) > tq_c
    tq_n = jnp.right_shift(jnp.bitwise_and(lax.broadcasted_iota(jnp.int32, (nrow, nn), 0), SUBLANES - 1), 1)
    col_n = lax.broadcasted_iota(jnp.int32, (nrow, nn), 1)
    vis_n = jnp.bitwise_and(col_n, SUBLANES - 1) <= tq_n
    half_n = jnp.right_shift(col_n, 3)
    s_n = s[:, 2 * WINDOW:2 * WINDOW + nn]
    pcs = []
    pn = jnp.zeros((nrow, nn), F32)
    for hh in range(2):
        sk = sink_ref[:, hh * LANES:hh * LANES + 1]
        sc = jnp.where(mask_c, s[:, hh * WINDOW:(hh + 1) * WINDOW], NEG)
        sn = jnp.where(vis_n & (half_n == hh), s_n, NEG)
        m = jnp.maximum(jnp.maximum(jnp.max(sc, axis=-1, keepdims=True), jnp.max(sn, axis=-1, keepdims=True)), sk)
        pc = jnp.exp(sc - m)
        ph = jnp.exp(sn - m)
        den = jnp.sum(pc, axis=-1, keepdims=True) + jnp.sum(ph, axis=-1, keepdims=True) + jnp.exp(sk - m)
        pcs.append(pc / den)
        pn = pn + ph / den
    pc = jnp.concatenate(pcs, axis=1)
    outs = []
    for k in range(nprob):
        rows = slice(k * SUBLANES, (k + 1) * SUBLANES)
        outs.append(lax.dot_general(pc[rows, :].astype(BF16), vcs[k], nt_dims, preferred_element_type=F32)
                    + _dot(pn[rows, :].astype(BF16), vns[k]))
    yb_ref[...] = jnp.concatenate(outs, axis=0)


def _attn_sample_call(l, q, kc, vc, kn, vn, sink_rows):
    bs = ATT_SEQS
    cache_in = pl.BlockSpec((None, bs, KV_HEADS, HEAD_DIM, WINDOW), lambda i: (l, i, 0, 0, 0))
    new_spec = pl.BlockSpec((bs * DEC_SEQ, KV_W), lambda i: (i, 0))
    q_spec = pl.BlockSpec((bs * 16, LANES), lambda i: (i, 0))
    return pl.pallas_call(
        _attn_sample_kernel,
        grid=(DEC_BATCH // bs,),
        in_specs=[q_spec, cache_in, cache_in, new_spec, new_spec, _layer_spec(l, (bs * 16, 2 * LANES))],
        out_specs=q_spec,
        out_shape=jax.ShapeDtypeStruct((DEC_BATCH * 16, LANES), F32),
        compiler_params=_cparams(("arbitrary",)),
        name="attn_sample",
    )(q, kc, vc, kn, vn, sink_rows)


def _rope_tables(pos):
    inv = ROPE_THETA ** (-jnp.arange(0, ROT_DIM, 2, dtype=F32) / ROT_DIM)
    ang = pos.astype(F32)[:, None] * inv[None, :]
    cos, sin = jnp.cos(ang), jnp.sin(ang)
    n = pos.shape[0]
    half = ROT_DIM // 2
    rest = HEAD_DIM - ROT_DIM
    z8 = jnp.zeros((n, half), F32)
    zr = jnp.zeros((n, rest), F32)
    c64 = jnp.concatenate([cos, cos, jnp.ones((n, rest), F32)], axis=1)
    sa64 = jnp.concatenate([z8, sin, zr], axis=1)
    sb64 = jnp.concatenate([-sin, z8, zr], axis=1)
    return tuple(jnp.tile(t, (1, LANES // HEAD_DIM)) for t in (c64, sa64, sb64))


def _ssm_params(a_re, a_im, log_dt, b_re, b_im, c_re, c_im):
    a_re, a_im = a_re.astype(F32), a_im.astype(F32)
    dt = jnp.exp(log_dt.astype(F32))[..., None]
    mag = jnp.exp(a_re * dt)
    abar_re, abar_im = mag * jnp.cos(a_im * dt), mag * jnp.sin(a_im * dt)
    inv = 1.0 / (a_re * a_re + a_im * a_im)
    f_re = (((abar_re - 1.0) * a_re + abar_im * a_im) * inv)[..., None]
    f_im = ((abar_im * a_re - (abar_re - 1.0) * a_im) * inv)[..., None]
    b_re, b_im = b_re.astype(F32), b_im.astype(F32)
    bbar_re = f_re * b_re - f_im * b_im
    bbar_im = f_re * b_im + f_im * b_re
    half = N_STATE // 2
    gpt = SSM_GROUPS // 2

    def diag_tiles(m):
        rr, cc = m.shape[2], m.shape[3]
        rep = jnp.tile(m.reshape(DEPTH, 2, gpt * rr, cc), (1, 1, 1, gpt))
        same = (jnp.arange(gpt * rr)[:, None] // rr) == (jnp.arange(gpt * cc)[None, :] // cc)
        return jnp.where(same, rep, 0.0).astype(BF16)

    diag_in = lambda m: diag_tiles(jnp.swapaxes(m, 2, 3))
    diag_out = lambda m: diag_tiles(jnp.swapaxes(m, 2, 3))

    return dict(
        ssm_a_re=abar_re.reshape(DEPTH, 1, N_STATE), ssm_a_im=abar_im.reshape(DEPTH, 1, N_STATE),
        ssm_b_re=diag_in(bbar_re), ssm_b_im=diag_in(bbar_im),
        ssm_c_re=diag_out(c_re.astype(F32)), ssm_c_imn=diag_out(-c_im.astype(F32)),
    )


def _pool_weight_pairs(pool_w):
    z = jnp.zeros((DEPTH, POOL_GROUP, POOL_GROUP), pool_w.dtype)
    pair = lambda a, b: jnp.concatenate([jnp.concatenate([a, z], axis=2), jnp.concatenate([z, b], axis=2)], axis=1)
    return jnp.stack([pair(pool_w[:, 0], pool_w[:, 1]), pair(pool_w[:, 2], pool_w[:, 3])], axis=1).astype(BF16)


def _sink_rows_sample(sinks):
    pairs = jnp.repeat(sinks.astype(F32).reshape(DEPTH, N_HEADS // 2, 2), LANES, axis=2)
    pairs = pairs.reshape(DEPTH, 1, KV_HEADS, 1, 2, 2 * LANES)
    rows = jnp.broadcast_to(pairs, (DEPTH, ATT_SEQS, KV_HEADS, DEC_SEQ, 2, 2 * LANES))
    return rows.reshape(DEPTH, ATT_SEQS * 16, 2 * LANES)


def kernel(x_prompt, x_sample, cache_win_k, cache_win_v, state_pool, state_ssm_re, state_ssm_im, c_prompt, c_sample, norm1_g, norm2_g, w_ada, b_ada, w_in, pool_w, pool_scale, attn_sinks, ssm_a_re, ssm_a_im, ssm_log_dt, ssm_b_re, ssm_b_im, ssm_c_re, ssm_c_im, ssm_d, w_glu, w_branch_a, w_branch_b, w_branch_c, w_out, w_ffn_in, w_ffn_out, final_norm_g):
    mod = _ada_call(jnp.concatenate([c_sample, c_prompt], axis=0), w_ada, b_ada)
    tabs_p = _rope_tables(jnp.arange(SEQ, dtype=jnp.int32))
    tabs_s = _rope_tables(PAST_LEN + jnp.arange(DEC_SEQ, dtype=jnp.int32))
    g_final = final_norm_g.reshape(1, D_MODEL)

    wts = dict(
        g1=norm1_g.reshape(DEPTH, 1, D_MODEL), g2=norm2_g.reshape(DEPTH, 1, D_MODEL),
        w_in=w_in.astype(BF16),
        pool_w=_pool_weight_pairs(pool_w), pool_scale=pool_scale.reshape(DEPTH, 1, W_A),
        ssm_d=ssm_d.reshape(DEPTH, 1, W_C), w_glu=w_glu.astype(BF16),
        w_a=w_branch_a.astype(BF16), w_b=w_branch_b.astype(BF16), w_c=w_branch_c.astype(BF16),
        w_out=w_out.astype(BF16), w_ffn_in=w_ffn_in.astype(BF16), w_ffn_out=w_ffn_out.astype(BF16),
        **_ssm_params(ssm_a_re, ssm_a_im, ssm_log_dt, ssm_b_re, ssm_b_im, ssm_c_re, ssm_c_im),
    )
    sinks_p = attn_sinks.astype(F32)
    sinks_s = _sink_rows_sample(attn_sinks)
    cache_kt = jnp.transpose(cache_win_k, (0, 1, 3, 4, 2))
    cache_vt = jnp.transpose(cache_win_v, (0, 1, 3, 4, 2))
    pool_prev = jnp.transpose(state_pool, (0, 2, 1, 3))
    h0_re = state_ssm_re.reshape(DEPTH, DEC_BATCH, N_STATE)
    h0_im = state_ssm_im.reshape(DEPTH, DEC_BATCH, N_STATE)
    zeros_state = jnp.zeros((1, BATCH, N_STATE), F32)

    hp = x_prompt
    hs = x_sample.reshape(DEC_BATCH, DEC_SEQ * D_MODEL)
    outs = [[] for _ in range(10)]
    for l in range(DEPTH):
        last = l == DEPTH - 1

        ya, q, kv, u, kwin, vwin, pool_new = _proj_prompt_call(l, hp, mod, wts, tabs_p)
        yb = _attn_prompt_call(l, q, kv, sinks_p)
        yc, hre, him = _ssm_call(l, u.reshape(BATCH, SEQ, W_C), zeros_state, zeros_state, 0, wts,
                                 BATCH, SSM_TC, True, "ssm_prompt")
        hp = _merge_call(l, False, hp, mod, ya, yb, yc.reshape(BATCH * SEQ, W_C), wts)
        hp = _ffn_call(l, False, last, hp, mod, wts, g_final)
        for k, v in zip(range(5), (kwin.reshape(BATCH, WINDOW, KV_HEADS, HEAD_DIM),
                                   vwin.reshape(BATCH, WINDOW, KV_HEADS, HEAD_DIM), pool_new,
                                   hre.reshape(BATCH, SSM_GROUPS, SSM_STATE),
                                   him.reshape(BATCH, SSM_GROUPS, SSM_STATE))):
            outs[k].append(v)

        ya, q, kn, vn, u, pool_new = _proj_sample_call(l, hs, mod, wts, tabs_s, pool_prev)
        yb = _attn_sample_call(l, q, cache_kt, cache_vt, kn, vn, sinks_s)
        yc, hre, him = _ssm_call(l, u, h0_re, h0_im, l, wts, DEC_BATCH, DEC_SEQ, False, "ssm_sample")
        hs = _merge_call(l, True, hs, mod, ya, yb, yc, wts)
        hs = _ffn_call(l, True, last, hs, mod, wts, g_final)
        for k, v in zip(range(5, 10), (kn.reshape(DEC_BATCH, DEC_SEQ, KV_HEADS, HEAD_DIM),
                                       vn.reshape(DEC_BATCH, DEC_SEQ, KV_HEADS, HEAD_DIM),
                                       pool_new,
                                       hre.reshape(DEC_BATCH, SSM_GROUPS, SSM_STATE),
                                       him.reshape(DEC_BATCH, SSM_GROUPS, SSM_STATE))):
            outs[k].append(v)

    stacked = [jnp.stack(o) for o in outs]
    stacked[5] = jnp.concatenate([cache_win_k[:, :, DEC_SEQ:], stacked[5]], axis=2)
    stacked[6] = jnp.concatenate([cache_win_v[:, :, DEC_SEQ:], stacked[6]], axis=2)
    stacked[7] = jnp.transpose(stacked[7], (0, 2, 1, 3))
    return (hp, hs.reshape(DEC_BATCH, DEC_SEQ, D_MODEL), *stacked)
```

```python
import functools

import jax
import jax.numpy as jnp
from jax import lax
from jax.experimental import pallas as pl
from jax.experimental.pallas import tpu as pltpu

F32 = jnp.float32
BF16 = jnp.bfloat16

D_MODEL = 1024
BATCH = 8
SEQ = 2048
DEPTH = 4
DEC_BATCH = 128
DEC_SEQ = 4
PAST_LEN = 8192

W_A = 512
POOL_WINDOWS = (2, 4, 8, 16)
POOL_GROUP = 128
POOL_BUF = 15
POOL_HIST = 32
HEAD_DIM = 64
N_HEADS = 8
KV_HEADS = 2
W_B = 512
KV_W = 128
WINDOW = 128
ROT_DIM = 16
ROPE_THETA = 500000.0
W_C = 512
SSM_CH = 16
SSM_GROUPS = 32
SSM_STATE = 64
N_STATE = SSM_GROUPS * SSM_STATE
D_FF = 2816
EPS = 1e-6
MIX_COLS = W_A + W_B + 2 * KV_W + W_C

LANES = 128
SUBLANES = 8
VMEM_LIMIT = 56 * 1024 * 1024

TM = 1024
NT = SEQ // TM
PROJ_SUB = 128
ATT_BLK = WINDOW
ATT_SUB = 8
TM_WIDE = 1024
ATT_SEQS = 16
SSM_TC = 128
SSM_SUB = 128
SCAN_CW = 512
NEG = -1e30
FF_CHUNKS = ((0, 1024), (1024, 1024), (2048, 768))
PROMPT_MOD_BLOCK = DEC_BATCH // BATCH


def _cparams(sem):
    return pltpu.CompilerParams(dimension_semantics=sem, vmem_limit_bytes=VMEM_LIMIT)


def _dot(a, b):
    return jnp.dot(a, b, preferred_element_type=F32)


def _norm_mod(x, g, sc, sh):
    r = lax.rsqrt(jnp.mean(x * x, axis=-1, keepdims=True) + EPS)
    return (x * r) * g * (1.0 + sc) + sh


def _mod_rows(ref, sample):
    return ref[...] if sample else ref[pl.ds(pl.program_id(0), 1), :]


def _rope(x, cos, sa, sb):
    return x * cos + pltpu.roll(x, 8, axis=1) * sa + pltpu.roll(x, LANES - 8, axis=1) * sb


def _layer_spec(l, tail, **kw):
    nz = (0,) * len(tail)
    return pl.BlockSpec((None, *tail), lambda *_: (l, *nz), **kw)


def _mod_spec(l, sample, j):
    if sample:
        return pl.BlockSpec((None, DEC_BATCH, D_MODEL), lambda *_: (l, 0, j))
    return pl.BlockSpec((None, BATCH, D_MODEL), lambda *_: (l, PROMPT_MOD_BLOCK, j))


def _ada_kernel(c_ref, w_ref, b_ref, o_ref):
    c = c_ref[...]
    a = (c * jax.nn.sigmoid(c)).astype(BF16)
    o_ref[...] = _dot(a, w_ref[...].astype(BF16)) + b_ref[...]


def _ada_call(c_all, w_ada, b_ada):
    n = c_all.shape[0]
    cb = 1536
    return pl.pallas_call(
        _ada_kernel,
        grid=(DEPTH, 6 * D_MODEL // cb),
        in_specs=[
            pl.BlockSpec((n, D_MODEL), lambda l, j: (0, 0)),
            pl.BlockSpec((None, D_MODEL, cb), lambda l, j: (l, 0, j)),
            pl.BlockSpec((None, 1, cb), lambda l, j: (l, 0, j)),
        ],
        out_specs=pl.BlockSpec((None, n, cb), lambda l, j: (l, 0, j)),
        out_shape=jax.ShapeDtypeStruct((DEPTH, n, 6 * D_MODEL), F32),
        compiler_params=_cparams(("arbitrary", "arbitrary")),
        name="ada",
    )(c_all, w_ada, b_ada.reshape(DEPTH, 1, 6 * D_MODEL))


def _proj_prompt_kernel(x_ref, sh_ref, sc_ref, g_ref, w_ref, pw_ref, ps_ref, cos_ref, sa_ref, sb_ref,
                        ya_ref, q_ref, kv_ref, u_ref, kwin_ref, vwin_ref, pool_ref, xe_ref, s2_ref, s4_ref, s8_ref):
    t = pl.program_id(1)
    tm = x_ref.shape[0]
    hist = POOL_HIST
    last = t == pl.num_programs(1) - 1
    sh = _mod_rows(sh_ref, False)
    sc = _mod_rows(sc_ref, False)
    g1, g2 = POOL_GROUP, 2 * POOL_GROUP

    @pl.when(t == 0)
    def _():
        xe_ref[0:hist, :] = jnp.zeros((hist, W_A), F32)

    for r0 in range(0, tm, PROJ_SUB):
        rows = slice(r0, r0 + PROJ_SUB)
        h = _norm_mod(x_ref[rows, :], g_ref[...], sc, sh).astype(BF16)
        z = _dot(h, w_ref[...])
        xa = z[:, 0:W_A]

        a = hist + r0
        e = a + PROJ_SUB
        xe_ref[a:e, :] = xa
        s2_ref[a - 24:e, :] = xe_ref[a - 24:e, :] + xe_ref[a - 25:e - 1, :]
        s4_ref[a - 16:e, :] = s2_ref[a - 16:e, g1:] + s2_ref[a - 18:e - 2, g1:]
        s8_ref[a - 8:e, :] = s4_ref[a - 8:e, g1:] + s4_ref[a - 12:e - 4, g1:]
        sums = (s2_ref[a:e, 0:g1], s4_ref[a:e, 0:g1], s8_ref[a:e, 0:g1],
                s8_ref[a:e, g1:g2] + s8_ref[a - 8:e - 8, g1:g2])
        pos = t * tm + r0 + lax.broadcasted_iota(jnp.int32, (PROJ_SUB, 1), 0)
        ds = []
        for g, w in enumerate(POOL_WINDOWS):
            cnt = jnp.minimum(pos + 1, w).astype(F32)
            ds.append((sums[g] / cnt - xa[:, g * POOL_GROUP:(g + 1) * POOL_GROUP]).astype(BF16))
        for k in range(2):
            y = _dot(jnp.concatenate(ds[2 * k:2 * k + 2], axis=1), pw_ref[k]) * ps_ref[:, k * g2:(k + 1) * g2]
            ya_ref[rows, k * g2:(k + 1) * g2] = y.astype(BF16)

        cos = cos_ref[rows, :]
        sa = sa_ref[rows, :]
        sb = sb_ref[rows, :]
        for c in range(W_B // LANES):
            qc = z[:, W_A + c * LANES:W_A + (c + 1) * LANES]
            q_ref[rows, c * LANES:(c + 1) * LANES] = (_rope(qc, cos, sa, sb) * (HEAD_DIM ** -0.5)).astype(BF16)
        kz = _rope(z[:, W_A + W_B:W_A + W_B + KV_W], cos, sa, sb)
        vz = z[:, W_A + W_B + KV_W:W_A + W_B + 2 * KV_W]
        kv_ref[rows, 0:128] = kz.astype(BF16)
        kv_ref[rows, 128:256] = pltpu.roll(kz, HEAD_DIM, axis=1).astype(BF16)
        kv_ref[rows, 256:384] = vz.astype(BF16)
        kv_ref[rows, 384:512] = pltpu.roll(vz, HEAD_DIM, axis=1).astype(BF16)
        if r0 + PROJ_SUB == tm:
            @pl.when(last)
            def _(kz=kz, vz=vz):
                kwin_ref[...] = kz[PROJ_SUB - WINDOW:, :]
                vwin_ref[...] = vz[PROJ_SUB - WINDOW:, :]

        u_ref[rows, :] = z[:, W_A + W_B + 2 * KV_W:MIX_COLS]

    @pl.when(last)
    def _():
        pool_ref[...] = xe_ref[hist + tm - POOL_BUF:hist + tm, :]

    xe_ref[0:hist, :] = xe_ref[tm:tm + hist, :]


def _proj_prompt_call(l, x, mod, wts, rope_tabs):
    rows = BATCH * SEQ
    row_blk = lambda b, t: (b * NT + t, 0)
    tab_spec = pl.BlockSpec((TM, LANES), lambda b, t: (t, 0))
    return pl.pallas_call(
        _proj_prompt_kernel,
        grid=(BATCH, NT),
        in_specs=[
            pl.BlockSpec((None, TM, D_MODEL), lambda b, t: (b, t, 0)),
            _mod_spec(l, False, 0), _mod_spec(l, False, 1),
            _layer_spec(l, (1, D_MODEL)),
            _layer_spec(0, (D_MODEL, MIX_COLS)),
            _layer_spec(l, (2, 2 * POOL_GROUP, 2 * POOL_GROUP)),
            _layer_spec(l, (1, W_A)),
            tab_spec, tab_spec, tab_spec,
        ],
        out_specs=[
            pl.BlockSpec((TM, W_A), row_blk),
            pl.BlockSpec((TM, W_B), row_blk),
            pl.BlockSpec((TM, 4 * KV_W), row_blk),
            pl.BlockSpec((TM, W_C), row_blk),
            pl.BlockSpec((None, WINDOW, KV_W), lambda b, t: (b, 0, 0)),
            pl.BlockSpec((None, WINDOW, KV_W), lambda b, t: (b, 0, 0)),
            pl.BlockSpec((None, POOL_BUF, W_A), lambda b, t: (b, 0, 0)),
        ],
        out_shape=[
            jax.ShapeDtypeStruct((rows, W_A), BF16),
            jax.ShapeDtypeStruct((rows, W_B), BF16),
            jax.ShapeDtypeStruct((rows, 4 * KV_W), BF16),
            jax.ShapeDtypeStruct((rows, W_C), F32),
            jax.ShapeDtypeStruct((BATCH, WINDOW, KV_W), F32),
            jax.ShapeDtypeStruct((BATCH, WINDOW, KV_W), F32),
            jax.ShapeDtypeStruct((BATCH, POOL_BUF, W_A), F32),
        ],
        scratch_shapes=[pltpu.VMEM((TM + POOL_HIST, W_A), F32),
                        pltpu.VMEM((TM + POOL_HIST, W_A), F32),
                        pltpu.VMEM((TM + POOL_HIST, W_A - POOL_GROUP), F32),
                        pltpu.VMEM((TM + POOL_HIST, W_A - 2 * POOL_GROUP), F32)],
        compiler_params=_cparams(("arbitrary", "arbitrary")),
        name="proj_prompt",
    )(x, mod, mod, wts["g1"], wts["w_in"], wts["pool_w"], wts["pool_scale"], *rope_tabs)


def _head_pair_operand(lo_mask, a, b):
    zero = jnp.zeros_like(a)
    return jnp.concatenate([jnp.where(lo_mask, a, zero), jnp.where(lo_mask, zero, b)], axis=0)


def _attn_window_block(l, q_ref, yb_ref, sink_ref, r0, kv_prev, kv_cur, prev_bias):
    n = ATT_BLK
    lo = lax.broadcasted_iota(jnp.int32, (n, LANES), 1) < HEAD_DIM
    rows = jnp.bitwise_and(lax.broadcasted_iota(jnp.int32, (2 * n, LANES), 0), n - 1)
    from_prev = lax.broadcasted_iota(jnp.int32, (2 * n, LANES), 1) > rows
    lo2 = lax.broadcasted_iota(jnp.int32, (2 * n, LANES), 1) < HEAD_DIM
    top = lax.broadcasted_iota(jnp.int32, (2 * n, 1), 0) < n
    zero = jnp.zeros((n, LANES), BF16)
    ones_a = jnp.where(lo, 1.0, 0.0).astype(BF16)
    ones_b = jnp.where(lo, 0.0, 1.0).astype(BF16)
    nt_dims = (((1,), (1,)), ((), ()))
    for g in range(KV_HEADS):
        ca, cb = (0, LANES) if g == 0 else (LANES, 0)
        kk = jnp.concatenate([jnp.where(lo, kv_prev[:, ca:ca + LANES], zero),
                              jnp.where(lo, kv_cur[:, ca:ca + LANES], zero),
                              jnp.where(lo, zero, kv_prev[:, cb:cb + LANES]),
                              jnp.where(lo, zero, kv_cur[:, cb:cb + LANES])], axis=0)
        va, vb = 2 * LANES + ca, 2 * LANES + cb
        vv = jnp.concatenate([
            jnp.concatenate([jnp.where(lo, kv_prev[:, va:va + LANES], zero), ones_a], axis=1),
            jnp.concatenate([jnp.where(lo, kv_cur[:, va:va + LANES], zero), ones_a], axis=1),
            jnp.concatenate([jnp.where(lo, zero, kv_prev[:, vb:vb + LANES]), ones_b], axis=1),
            jnp.concatenate([jnp.where(lo, zero, kv_cur[:, vb:vb + LANES]), ones_b], axis=1)], axis=0)
        qs = jnp.concatenate([q_ref[pl.ds(r0, n), (2 * g) * LANES:(2 * g + 1) * LANES],
                              q_ref[pl.ds(r0, n), (2 * g + 1) * LANES:(2 * g + 2) * LANES]], axis=0)
        s = lax.dot_general(qs, kk, nt_dims, preferred_element_type=F32)
        ps, es = [], []
        for hh in range(2):
            s_prev = s[:, hh * 2 * n:hh * 2 * n + n]
            if prev_bias is not None:
                s_prev = s_prev + prev_bias
            logits = jnp.where(from_prev, s_prev, s[:, hh * 2 * n + n:(hh + 1) * 2 * n])
            sk = jnp.where(top, sink_ref[l, 4 * g + hh], sink_ref[l, 4 * g + 2 + hh])
            m = jnp.maximum(jnp.max(logits, axis=-1, keepdims=True), sk)
            p = jnp.exp(logits - m)
            ps += [jnp.where(from_prev, p, 0.0), jnp.where(from_prev, 0.0, p)]
            es.append(jnp.exp(sk - m))
        r = _dot(jnp.concatenate(ps, axis=1).astype(BF16), vv)
        o = r[:, 0:LANES] / (r[:, LANES:2 * LANES] + jnp.where(lo2, es[0], es[1]))
        yb_ref[pl.ds(r0, n), (2 * g) * LANES:(2 * g + 1) * LANES] = o[0:n].astype(BF16)
        yb_ref[pl.ds(r0, n), (2 * g + 1) * LANES:(2 * g + 2) * LANES] = o[n:2 * n].astype(BF16)


def _attn_prompt_kernel(l, sink_ref, q_ref, kvc_ref, kvp_ref, yb_ref):
    i = pl.program_id(1)
    first_bias = jnp.where(i > 0, 0.0, NEG).astype(F32)
    for sub in range(ATT_SUB):
        r0 = sub * ATT_BLK
        kv_cur = kvc_ref[r0:r0 + ATT_BLK, :]
        kv_prev = kvp_ref[...] if sub == 0 else kvc_ref[r0 - ATT_BLK:r0, :]
        _attn_window_block(l, q_ref, yb_ref, sink_ref, r0, kv_prev, kv_cur, first_bias if sub == 0 else None)


def _attn_prompt_call(l, q, kv, sinks):
    rows = ATT_SUB * ATT_BLK
    nb = SEQ // rows
    return pl.pallas_call(
        functools.partial(_attn_prompt_kernel, l),
        grid=(BATCH, nb),
        in_specs=[
            pl.BlockSpec(memory_space=pltpu.SMEM),
            pl.BlockSpec((rows, W_B), lambda b, i: (b * nb + i, 0)),
            pl.BlockSpec((rows, 4 * KV_W), lambda b, i: (b * nb + i, 0)),
            pl.BlockSpec((ATT_BLK, 4 * KV_W), lambda b, i: ((b * nb + i) * ATT_SUB - jnp.minimum(i, 1), 0)),
        ],
        out_specs=pl.BlockSpec((rows, W_B), lambda b, i: (b * nb + i, 0)),
        out_shape=jax.ShapeDtypeStruct((BATCH * SEQ, W_B), BF16),
        compiler_params=_cparams(("arbitrary", "arbitrary")),
        name="attn_prompt",
    )(sinks, q, kv, kv)


def _ssm_kernel(rows_per_step, seq_major, u_ref, h0re_ref, h0im_ref, are_ref, aim_ref, bre_ref, bim_ref, cre_ref,
                cimn_ref, d_ref, wglu_ref, yc_ref, hre_out, him_out, dre, dim_, hre, him, *slabs):
    i = pl.program_id(0)
    r = rows_per_step
    half = N_STATE // 2
    nslab = W_C // LANES
    blk = dre.shape[0]
    steps = SSM_SUB // r

    @pl.when(i == 0)
    def _():
        hre[...] = h0re_ref[...]
        him[...] = h0im_ref[...]

    if seq_major:
        (perm,) = slabs
        tc = u_ref.shape[1]
        for b in range(r):
            for c in range(nslab):
                perm[c, pl.ds(b, tc, stride=r), :] = u_ref[b, :, c * LANES:(c + 1) * LANES]

    def u_rows(rows):
        if seq_major:
            return jnp.concatenate([perm[c, rows, :] for c in range(nslab)], axis=1)
        return u_ref[rows, :]

    def drive(s):
        rows = slice(s * SSM_SUB, (s + 1) * SSM_SUB)
        ub = u_rows(rows).astype(BF16)
        for kt in range(2):
            uk = ub[:, kt * 256:(kt + 1) * 256]
            dre[rows, kt * half:(kt + 1) * half] = _dot(uk, bre_ref[kt])
            dim_[rows, kt * half:(kt + 1) * half] = _dot(uk, bim_ref[kt])

    def scan(s):
        for rt in range(r // SUBLANES):
            rsl = slice(rt * SUBLANES, (rt + 1) * SUBLANES)
            for c in range(N_STATE // SCAN_CW):
                cols = slice(c * SCAN_CW, (c + 1) * SCAN_CW)
                ar = jnp.broadcast_to(are_ref[:, cols], (SUBLANES, SCAN_CW))
                ai = jnp.broadcast_to(aim_ref[:, cols], (SUBLANES, SCAN_CW))
                pr, pi = hre[rsl, cols], him[rsl, cols]
                for t in range(steps):
                    row = s * SSM_SUB + t * r + rt * SUBLANES
                    trow = slice(row, row + SUBLANES)
                    nr = ar * pr - ai * pi + dre[trow, cols]
                    ni = ar * pi + ai * pr + dim_[trow, cols]
                    dre[trow, cols] = nr
                    dim_[trow, cols] = ni
                    pr, pi = nr, ni
                hre[rsl, cols] = pr
                him[rsl, cols] = pi

    def project(s):
        rows = slice(s * SSM_SUB, (s + 1) * SSM_SUB)
        parts = []
        for nt in range(2):
            sl = slice(nt * half, (nt + 1) * half)
            parts.append(_dot(dre[rows, sl].astype(BF16), cre_ref[nt]) + _dot(dim_[rows, sl].astype(BF16), cimn_ref[nt]))
        return jnp.concatenate(parts, axis=1)

    nsub = blk // SSM_SUB
    drive(0)
    ys = []
    for s in range(nsub):
        if s + 1 < nsub:
            drive(s + 1)
        scan(s)
        ys.append(project(s))

    @pl.when(i == pl.num_programs(0) - 1)
    def _():
        hre_out[...] = hre[...]
        him_out[...] = him[...]

    y = jnp.concatenate(ys, axis=0)
    if seq_major:
        for c in range(nslab):
            perm[c] = y[:, c * LANES:(c + 1) * LANES]
        y = jnp.concatenate(
            [jnp.concatenate([perm[c, pl.ds(b, tc, stride=r), :] for c in range(nslab)], axis=1) for b in range(r)],
            axis=0)
        u = u_ref[...].reshape(r * tc, W_C)
    else:
        u = u_ref[...]
    y = jax.nn.gelu(y + d_ref[...] * u)
    yc = (y * jax.nn.sigmoid(_dot(y.astype(BF16), wglu_ref[...]))).astype(BF16)
    yc_ref[...] = yc.reshape(yc_ref.shape)


def _ssm_call(l, u, h0re, h0im, h0_layer, wts, rows_per_step, steps_per_block, seq_major, name):
    blk = rows_per_step * steps_per_block
    half = N_STATE // 2
    if seq_major:
        nsteps = u.shape[1] // steps_per_block
        io_spec = pl.BlockSpec((rows_per_step, steps_per_block, W_C), lambda i: (0, i, 0))
        scratch = [pltpu.VMEM((W_C // LANES, blk, LANES), F32)]
    else:
        nsteps = u.shape[0] // blk
        io_spec = pl.BlockSpec((blk, W_C), lambda i: (i, 0))
        scratch = []
    state_spec = pl.BlockSpec((rows_per_step, N_STATE), lambda i: (0, 0))
    h0_spec = _layer_spec(h0_layer, (rows_per_step, N_STATE))
    return pl.pallas_call(
        functools.partial(_ssm_kernel, rows_per_step, seq_major),
        grid=(nsteps,),
        in_specs=[
            io_spec, h0_spec, h0_spec,
            _layer_spec(l, (1, N_STATE)), _layer_spec(l, (1, N_STATE)),
            _layer_spec(l, (2, 256, half)), _layer_spec(l, (2, 256, half)),
            _layer_spec(l, (2, half, 256)), _layer_spec(l, (2, half, 256)),
            _layer_spec(l, (1, W_C)),
            _layer_spec(0, (W_C, W_C)),
        ],
        out_specs=[io_spec, state_spec, state_spec],
        out_shape=[
            jax.ShapeDtypeStruct(u.shape, BF16),
            jax.ShapeDtypeStruct((rows_per_step, N_STATE), F32),
            jax.ShapeDtypeStruct((rows_per_step, N_STATE), F32),
        ],
        scratch_shapes=[
            pltpu.VMEM((blk, N_STATE), F32),
            pltpu.VMEM((blk, N_STATE), F32),
            pltpu.VMEM((rows_per_step, N_STATE), F32),
            pltpu.VMEM((rows_per_step, N_STATE), F32),
        ] + scratch,
        compiler_params=_cparams(("arbitrary",)),
        name=name,
    )(u, h0re, h0im, wts["ssm_a_re"], wts["ssm_a_im"], wts["ssm_b_re"], wts["ssm_b_im"], wts["ssm_c_re"],
      wts["ssm_c_imn"], wts["ssm_d"], wts["w_glu"])


def _merge_kernel(sample, ncast, x_ref, sh_ref, sc_ref, gt_ref, g_ref, ya_ref, yb_ref, yc_ref,
                  wg_ref, wa_ref, wb_ref, wc_ref, wo_ref, *rest):
    cast_in, o_ref, cast_out = rest[:ncast], rest[ncast], rest[ncast + 1:]
    for src, dst in zip(cast_in, cast_out):
        dst[...] = src[...].astype(BF16)
    x = x_ref[...]
    h = _norm_mod(x, g_ref[...], _mod_rows(sc_ref, sample), _mod_rows(sh_ref, sample)).astype(BF16)
    if sample:
        t = pl.program_id(1)
        parts = []
        for j in range(N_HEADS // 2):
            start = (j // 2) * 8 + t * 2 + (j % 2)
            parts.append(yb_ref[pl.ds(start, DEC_BATCH, stride=16), :])
        yb = jnp.concatenate(parts, axis=1).astype(BF16)
    else:
        yb = yb_ref[...]
    merged = None
    for k, (y, w_ref) in enumerate(((ya_ref[...], wa_ref), (yb, wb_ref), (yc_ref[...], wc_ref))):
        gate = jax.nn.sigmoid(_dot(h, wg_ref[0, :, k * D_MODEL:(k + 1) * D_MODEL]))
        term = gate * _dot(y, w_ref[...])
        merged = term if merged is None else merged + term
    o_ref[...] = x + _mod_rows(gt_ref, sample) * _dot(merged.astype(BF16), wo_ref[...])


def _row_specs(sample):
    if sample:
        tm = DEC_BATCH
        grid = (1, DEC_SEQ)
        x_spec = pl.BlockSpec((tm, D_MODEL), lambda b, t: (0, t))
        row_spec = lambda cols: pl.BlockSpec((tm, cols), lambda b, t: (t, 0))
    else:
        tm = TM_WIDE
        nt = SEQ // tm
        grid = (BATCH, nt)
        x_spec = pl.BlockSpec((None, tm, D_MODEL), lambda b, t: (b, t, 0))
        row_spec = lambda cols: pl.BlockSpec((tm, cols), lambda b, t: (b * nt + t, 0))
    return grid, x_spec, row_spec


def _merge_call(l, sample, x, mod, ya, yb, yc, wts, cast_src=None):
    grid, x_spec, row_spec = _row_specs(sample)
    yb_spec = pl.BlockSpec(yb.shape, lambda b, t: (0, 0)) if sample else row_spec(W_B)
    single = pl.Buffered(1)
    gate_spec = pl.BlockSpec((pl.Element(1), pl.Element(D_MODEL), pl.Element(3 * D_MODEL)),
                             lambda *_: (0, 0, MIX_COLS), pipeline_mode=single)
    cast_keys = tuple(cast_src) if cast_src else ()
    nsteps = grid[0] * grid[1]
    cast_in, cast_specs_in, cast_specs_out, cast_shapes = [], [], [], []
    for key in cast_keys:
        _, rows, cols = cast_src[key].shape
        chunk = rows // nsteps
        cast_in.append(cast_src[key])
        cast_specs_in.append(pl.BlockSpec((None, chunk, cols), lambda b, t: (l + 1, b * grid[1] + t, 0)))
        cast_specs_out.append(pl.BlockSpec((None, chunk, cols), lambda b, t: (0, b * grid[1] + t, 0)))
        cast_shapes.append(jax.ShapeDtypeStruct((1, rows, cols), BF16))
    res = pl.pallas_call(
        functools.partial(_merge_kernel, sample, len(cast_keys)),
        grid=grid,
        in_specs=[
            x_spec, _mod_spec(l, sample, 0), _mod_spec(l, sample, 1), _mod_spec(l, sample, 2),
            _layer_spec(l, (1, D_MODEL)),
            row_spec(W_A), yb_spec, row_spec(W_C),
            gate_spec,
            _layer_spec(0, (W_A, D_MODEL), pipeline_mode=single),
            _layer_spec(0, (W_B, D_MODEL), pipeline_mode=single),
            _layer_spec(0, (W_C, D_MODEL), pipeline_mode=single),
            _layer_spec(0, (D_MODEL, D_MODEL), pipeline_mode=single),
        ] + cast_specs_in,
        out_specs=[x_spec] + cast_specs_out,
        out_shape=[jax.ShapeDtypeStruct(x.shape, F32)] + cast_shapes,
        compiler_params=_cparams(("arbitrary", "arbitrary")),
        name="merge_sample" if sample else "merge_prompt",
    )(x, mod, mod, mod, wts["g1"], ya, yb, yc, wts["w_in"], wts["w_a"], wts["w_b"], wts["w_c"], wts["w_out"],
      *cast_in)
    return res[0], dict(zip(cast_keys, res[1:]))


def _ffn_kernel(sample, final, x_ref, sh_ref, sc_ref, gt_ref, g_ref, wi_ref, wo_ref, gf_ref, o_ref):
    x = x_ref[...]
    h = _norm_mod(x, g_ref[...], _mod_rows(sc_ref, sample), _mod_rows(sh_ref, sample)).astype(BF16)
    acc = None
    for lo, n in FF_CHUNKS:
        a = _dot(h, wi_ref[:, lo:lo + n])
        b = _dot(h, wi_ref[:, D_FF + lo:D_FF + lo + n])
        act = ((a * jax.nn.sigmoid(a)) * b).astype(BF16)
        part = _dot(act, wo_ref[lo:lo + n, :])
        acc = part if acc is None else acc + part
    y = x + _mod_rows(gt_ref, sample) * acc
    if final:
        r = lax.rsqrt(jnp.mean(y * y, axis=-1, keepdims=True) + EPS)
        y = (y * r) * gf_ref[...]
    o_ref[...] = y


def _ffn_call(l, sample, final, x, mod, wts, g_final):
    grid, x_spec, _ = _row_specs(sample)
    single = pl.Buffered(1)
    return pl.pallas_call(
        functools.partial(_ffn_kernel, sample, final),
        grid=grid,
        in_specs=[
            x_spec, _mod_spec(l, sample, 3), _mod_spec(l, sample, 4), _mod_spec(l, sample, 5),
            _layer_spec(l, (1, D_MODEL)),
            _layer_spec(0, (D_MODEL, 2 * D_FF), pipeline_mode=single),
            _layer_spec(0, (D_FF, D_MODEL), pipeline_mode=single),
            pl.BlockSpec((1, D_MODEL), lambda b, t: (0, 0)),
        ],
        out_specs=x_spec,
        out_shape=jax.ShapeDtypeStruct(x.shape, F32),
        compiler_params=_cparams(("arbitrary", "arbitrary")),
        name="ffn_sample" if sample else "ffn_prompt",
    )(x, mod, mod, mod, wts["g2"], wts["w_ffn_in"], wts["w_ffn_out"], g_final)


def _proj_sample_kernel(x_ref, sh_ref, sc_ref, g_ref, w_ref, pw_ref, ps_ref, cos_ref, sa_ref, sb_ref, pool_prev_ref,
                        ya_ref, q_ref, kn_ref, vn_ref, u_ref, pool_ref):
    nb = DEC_BATCH
    sh = sh_ref[...]
    sc = sc_ref[...]
    h = jnp.concatenate(
        [_norm_mod(x_ref[:, t * D_MODEL:(t + 1) * D_MODEL], g_ref[...], sc, sh) for t in range(DEC_SEQ)],
        axis=0).astype(BF16)

    z = _dot(h, w_ref[...])
    xa = z[:, 0:W_A]
    xe = [pool_prev_ref[j] for j in range(POOL_BUF)]
    xe += [xa[t * nb:(t + 1) * nb, :] for t in range(DEC_SEQ)]
    for j in range(POOL_BUF):
        pool_ref[j] = xe[DEC_SEQ + j]
    dgs = []
    for g, w in enumerate(POOL_WINDOWS):
        lo = g * POOL_GROUP
        ds = []
        for t in range(DEC_SEQ):
            s = xe[POOL_BUF + t][:, lo:lo + POOL_GROUP]
            for j in range(1, w):
                s = s + xe[POOL_BUF + t - j][:, lo:lo + POOL_GROUP]
            ds.append(s / float(w) - xe[POOL_BUF + t][:, lo:lo + POOL_GROUP])
        dgs.append(jnp.concatenate(ds, axis=0).astype(BF16))
    for k in range(2):
        cols = slice(2 * k * POOL_GROUP, 2 * (k + 1) * POOL_GROUP)
        y = _dot(jnp.concatenate(dgs[2 * k:2 * k + 2], axis=1), pw_ref[k]) * ps_ref[:, cols]
        ya_ref[:, cols] = y.astype(BF16)

    def tabs(t):
        return cos_ref[t:t + 1, :], sa_ref[t:t + 1, :], sb_ref[t:t + 1, :]

    for j in range(W_B // LANES):
        qc = z[:, W_A + j * LANES:W_A + (j + 1) * LANES]
        for t in range(DEC_SEQ):
            qt = _rope(qc[t * nb:(t + 1) * nb, :], *tabs(t)) * (HEAD_DIM ** -0.5)
            q_ref[pl.ds((j // 2) * 8 + t * 2 + (j % 2), nb, stride=16), :] = qt
    kz = z[:, W_A + W_B:W_A + W_B + KV_W]
    vz = z[:, W_A + W_B + KV_W:W_A + W_B + 2 * KV_W]
    for t in range(DEC_SEQ):
        kn_ref[pl.ds(t, nb, stride=DEC_SEQ), :] = _rope(kz[t * nb:(t + 1) * nb, :], *tabs(t))
        vn_ref[pl.ds(t, nb, stride=DEC_SEQ), :] = vz[t * nb:(t + 1) * nb, :]
    u_ref[...] = z[:, W_A + W_B + 2 * KV_W:MIX_COLS]


def _proj_sample_call(l, x2d, mod, wts, rope_tabs, pool_prev):
    rows = DEC_BATCH * DEC_SEQ
    full = lambda shape: pl.BlockSpec(shape, lambda i: (0,) * len(shape))
    out_shape = [
        jax.ShapeDtypeStruct((rows, W_A), BF16),
        jax.ShapeDtypeStruct((DEC_BATCH * 16, LANES), F32),
        jax.ShapeDtypeStruct((rows, KV_W), F32),
        jax.ShapeDtypeStruct((rows, KV_W), F32),
        jax.ShapeDtypeStruct((rows, W_C), F32),
        jax.ShapeDtypeStruct((POOL_BUF, DEC_BATCH, W_A), F32),
    ]
    return pl.pallas_call(
        _proj_sample_kernel,
        grid=(1,),
        in_specs=[
            full(x2d.shape), _mod_spec(l, True, 0), _mod_spec(l, True, 1),
            _layer_spec(l, (1, D_MODEL)),
            _layer_spec(0, (D_MODEL, MIX_COLS)),
            _layer_spec(l, (2, 2 * POOL_GROUP, 2 * POOL_GROUP)),
            _layer_spec(l, (1, W_A)),
            full(rope_tabs[0].shape), full(rope_tabs[1].shape), full(rope_tabs[2].shape),
            _layer_spec(l, (POOL_BUF, DEC_BATCH, W_A)),
        ],
        out_specs=[full(s.shape) for s in out_shape],
        out_shape=out_shape,
        compiler_params=_cparams(("arbitrary",)),
        name="proj_sample",
    )(x2d, mod, mod, wts["g1"], wts["w_in"], wts["pool_w"], wts["pool_scale"], *rope_tabs, pool_prev)


def _attn_sample_kernel(q_ref, kc_ref, vc_ref, kn_ref, vn_ref, sink_ref, yb_ref):
    nn = 2 * SUBLANES
    nprob = KV_HEADS * ATT_SEQS
    nrow = nprob * SUBLANES
    lo_n = lax.broadcasted_iota(jnp.int32, (SUBLANES, LANES), 1) < HEAD_DIM
    pad = jnp.zeros((SUBLANES - DEC_SEQ, LANES), F32)
    zero = jnp.zeros((HEAD_DIM, WINDOW), BF16)
    nt_dims = (((1,), (1,)), ((), ()))

    def cache_operand(ref, i, g):
        m = ref[i, g].astype(BF16)
        return jnp.concatenate([jnp.concatenate([m, zero], axis=1), jnp.concatenate([zero, m], axis=1)], axis=0)

    def new_operands(new):
        new8 = jnp.concatenate([new, pad], axis=0)
        nr = pltpu.roll(new8, HEAD_DIM, axis=1)
        return [_head_pair_operand(lo_n, *((new8, nr) if g == 0 else (nr, new8))).astype(BF16)
                for g in range(KV_HEADS)]

    kcs = [cache_operand(kc_ref, i, g) for i in range(ATT_SEQS) for g in range(KV_HEADS)]
    vcs = [cache_operand(vc_ref, i, g) for i in range(ATT_SEQS) for g in range(KV_HEADS)]
    kns, vns = [], []
    for i in range(ATT_SEQS):
        kns += new_operands(kn_ref[i * DEC_SEQ:(i + 1) * DEC_SEQ, :])
        vns += new_operands(vn_ref[i * DEC_SEQ:(i + 1) * DEC_SEQ, :])

    qs = [q_ref[k * SUBLANES:(k + 1) * SUBLANES, :].astype(BF16) for k in range(nprob)]
    s = jnp.concatenate(
        [jnp.concatenate([_dot(qs[k], kcs[k]),
                          lax.dot_general(qs[k], kns[k], nt_dims, preferred_element_type=F32)], axis=1)
         for k in range(nprob)], axis=0)
    tq_c = jnp.right_shift(jnp.bitwise_and(lax.broadcasted_iota(jnp.int32, (nrow, WINDOW), 0), SUBLANES - 1), 1)
    mask_c = lax.broadcasted_iota(jnp.int32, (nrow, WINDOW), 1) > tq_c
    tq_n = jnp.right_shift(jnp.bitwise_and(lax.broadcasted_iota(jnp.int32, (nrow, nn), 0), SUBLANES - 1), 1)
    col_n = lax.broadcasted_iota(jnp.int32, (nrow, nn), 1)
    vis_n = jnp.bitwise_and(col_n, SUBLANES - 1) <= tq_n
    half_n = jnp.right_shift(col_n, 3)
    s_n = s[:, 2 * WINDOW:2 * WINDOW + nn]
    pcs = []
    pn = jnp.zeros((nrow, nn), F32)
    for hh in range(2):
        sk = sink_ref[:, hh * LANES:hh * LANES + 1]
        sc = jnp.where(mask_c, s[:, hh * WINDOW:(hh + 1) * WINDOW], NEG)
        sn = jnp.where(vis_n & (half_n == hh), s_n, NEG)
        m = jnp.maximum(jnp.maximum(jnp.max(sc, axis=-1, keepdims=True), jnp.max(sn, axis=-1, keepdims=True)), sk)
        pc = jnp.exp(sc - m)
        ph = jnp.exp(sn - m)
        den = jnp.sum(pc, axis=-1, keepdims=True) + jnp.sum(ph, axis=-1, keepdims=True) + jnp.exp(sk - m)
        pcs.append(pc / den)
        pn = pn + ph / den
    pc = jnp.concatenate(pcs, axis=1)
    outs = []
    for k in range(nprob):
        rows = slice(k * SUBLANES, (k + 1) * SUBLANES)
        outs.append(lax.dot_general(pc[rows, :].astype(BF16), vcs[k], nt_dims, preferred_element_type=F32)
                    + _dot(pn[rows, :].astype(BF16), vns[k]))
    yb_ref[...] = jnp.concatenate(outs, axis=0)


def _attn_sample_call(l, q, kc, vc, kn, vn, sink_rows):
    bs = ATT_SEQS
    cache_in = pl.BlockSpec((None, bs, KV_HEADS, HEAD_DIM, WINDOW), lambda i: (l, i, 0, 0, 0))
    new_spec = pl.BlockSpec((bs * DEC_SEQ, KV_W), lambda i: (i, 0))
    q_spec = pl.BlockSpec((bs * 16, LANES), lambda i: (i, 0))
    return pl.pallas_call(
        _attn_sample_kernel,
        grid=(DEC_BATCH // bs,),
        in_specs=[q_spec, cache_in, cache_in, new_spec, new_spec, _layer_spec(l, (bs * 16, 2 * LANES))],
        out_specs=q_spec,
        out_shape=jax.ShapeDtypeStruct((DEC_BATCH * 16, LANES), F32),
        compiler_params=_cparams(("arbitrary",)),
        name="attn_sample",
    )(q, kc, vc, kn, vn, sink_rows)


def _rope_tables(pos):
    inv = ROPE_THETA ** (-jnp.arange(0, ROT_DIM, 2, dtype=F32) / ROT_DIM)
    ang = pos.astype(F32)[:, None] * inv[None, :]
    cos, sin = jnp.cos(ang), jnp.sin(ang)
    n = pos.shape[0]
    half = ROT_DIM // 2
    rest = HEAD_DIM - ROT_DIM
    z8 = jnp.zeros((n, half), F32)
    zr = jnp.zeros((n, rest), F32)
    c64 = jnp.concatenate([cos, cos, jnp.ones((n, rest), F32)], axis=1)
    sa64 = jnp.concatenate([z8, sin, zr], axis=1)
    sb64 = jnp.concatenate([-sin, z8, zr], axis=1)
    return tuple(jnp.tile(t, (1, LANES // HEAD_DIM)) for t in (c64, sa64, sb64))


def _ssm_params(a_re, a_im, log_dt, b_re, b_im, c_re, c_im):
    a_re, a_im = a_re.astype(F32), a_im.astype(F32)
    dt = jnp.exp(log_dt.astype(F32))[..., None]
    mag = jnp.exp(a_re * dt)
    abar_re, abar_im = mag * jnp.cos(a_im * dt), mag * jnp.sin(a_im * dt)
    inv = 1.0 / (a_re * a_re + a_im * a_im)
    f_re = (((abar_re - 1.0) * a_re + abar_im * a_im) * inv)[..., None]
    f_im = ((abar_im * a_re - (abar_re - 1.0) * a_im) * inv)[..., None]
    b_re, b_im = b_re.astype(F32), b_im.astype(F32)
    bbar_re = f_re * b_re - f_im * b_im
    bbar_im = f_re * b_im + f_im * b_re
    half = N_STATE // 2
    gpt = SSM_GROUPS // 2

    def diag_tiles(m):
        rr, cc = m.shape[2], m.shape[3]
        rep = jnp.tile(m.reshape(DEPTH, 2, gpt * rr, cc), (1, 1, 1, gpt))
        same = (jnp.arange(gpt * rr)[:, None] // rr) == (jnp.arange(gpt * cc)[None, :] // cc)
        return jnp.where(same, rep, 0.0).astype(BF16)

    diag_in = lambda m: diag_tiles(jnp.swapaxes(m, 2, 3))
    diag_out = lambda m: diag_tiles(jnp.swapaxes(m, 2, 3))

    return dict(
        ssm_a_re=abar_re.reshape(DEPTH, 1, N_STATE), ssm_a_im=abar_im.reshape(DEPTH, 1, N_STATE),
        ssm_b_re=diag_in(bbar_re), ssm_b_im=diag_in(bbar_im),
        ssm_c_re=diag_out(c_re.astype(F32)), ssm_c_imn=diag_out(-c_im.astype(F32)),
    )


def _pool_weight_pairs(pool_w):
    z = jnp.zeros((DEPTH, POOL_GROUP, POOL_GROUP), pool_w.dtype)
    pair = lambda a, b: jnp.concatenate([jnp.concatenate([a, z], axis=2), jnp.concatenate([z, b], axis=2)], axis=1)
    return jnp.stack([pair(pool_w[:, 0], pool_w[:, 1]), pair(pool_w[:, 2], pool_w[:, 3])], axis=1).astype(BF16)


def _sink_rows_sample(sinks):
    pairs = jnp.repeat(sinks.astype(F32).reshape(DEPTH, N_HEADS // 2, 2), LANES, axis=2)
    pairs = pairs.reshape(DEPTH, 1, KV_HEADS, 1, 2, 2 * LANES)
    rows = jnp.broadcast_to(pairs, (DEPTH, ATT_SEQS, KV_HEADS, DEC_SEQ, 2, 2 * LANES))
    return rows.reshape(DEPTH, ATT_SEQS * 16, 2 * LANES)


def kernel(x_prompt, x_sample, cache_win_k, cache_win_v, state_pool, state_ssm_re, state_ssm_im, c_prompt, c_sample, norm1_g, norm2_g, w_ada, b_ada, w_in, pool_w, pool_scale, attn_sinks, ssm_a_re, ssm_a_im, ssm_log_dt, ssm_b_re, ssm_b_im, ssm_c_re, ssm_c_im, ssm_d, w_glu, w_branch_a, w_branch_b, w_branch_c, w_out, w_ffn_in, w_ffn_out, final_norm_g):
    mod = _ada_call(jnp.concatenate([c_sample, c_prompt], axis=0), w_ada, b_ada)
    tabs_p = _rope_tables(jnp.arange(SEQ, dtype=jnp.int32))
    tabs_s = _rope_tables(PAST_LEN + jnp.arange(DEC_SEQ, dtype=jnp.int32))
    g_final = final_norm_g.reshape(1, D_MODEL)

    small = dict(
        g1=norm1_g.reshape(DEPTH, 1, D_MODEL), g2=norm2_g.reshape(DEPTH, 1, D_MODEL),
        pool_w=_pool_weight_pairs(pool_w), pool_scale=pool_scale.reshape(DEPTH, 1, W_A),
        ssm_d=ssm_d.reshape(DEPTH, 1, W_C),
        **_ssm_params(ssm_a_re, ssm_a_im, ssm_log_dt, ssm_b_re, ssm_b_im, ssm_c_re, ssm_c_im),
    )
    big_f32 = dict(w_in=w_in, w_glu=w_glu, w_a=w_branch_a, w_b=w_branch_b, w_c=w_branch_c, w_out=w_out,
                   w_ffn_in=w_ffn_in, w_ffn_out=w_ffn_out)
    big = {k: v[0:1].astype(BF16) for k, v in big_f32.items()}
    sinks_p = attn_sinks.astype(F32)
    sinks_s = _sink_rows_sample(attn_sinks)
    cache_kt = jnp.transpose(cache_win_k, (0, 1, 3, 4, 2))
    cache_vt = jnp.transpose(cache_win_v, (0, 1, 3, 4, 2))
    pool_prev = jnp.transpose(state_pool, (0, 2, 1, 3))
    h0_re = state_ssm_re.reshape(DEPTH, DEC_BATCH, N_STATE)
    h0_im = state_ssm_im.reshape(DEPTH, DEC_BATCH, N_STATE)
    zeros_state = jnp.zeros((1, BATCH, N_STATE), F32)

    hp = x_prompt
    hs = x_sample.reshape(DEC_BATCH, DEC_SEQ * D_MODEL)
    outs = [[] for _ in range(10)]
    for l in range(DEPTH):
        last = l == DEPTH - 1
        wts = {**small, **big}

        ya, q, kv, u, kwin, vwin, pool_new = _proj_prompt_call(l, hp, mod, wts, tabs_p)
        yb = _attn_prompt_call(l, q, kv, sinks_p)
        yc, hre, him = _ssm_call(l, u.reshape(BATCH, SEQ, W_C), zeros_state, zeros_state, 0, wts,
                                 BATCH, SSM_TC, True, "ssm_prompt")
        hp, big = _merge_call(l, False, hp, mod, ya, yb, yc.reshape(BATCH * SEQ, W_C), wts,
                              cast_src=None if last else big_f32)
        hp = _ffn_call(l, False, last, hp, mod, wts, g_final)
        for k, v in zip(range(5), (kwin.reshape(BATCH, WINDOW, KV_HEADS, HEAD_DIM),
                                   vwin.reshape(BATCH, WINDOW, KV_HEADS, HEAD_DIM), pool_new,
                                   hre.reshape(BATCH, SSM_GROUPS, SSM_STATE),
                                   him.reshape(BATCH, SSM_GROUPS, SSM_STATE))):
            outs[k].append(v)

        ya, q, kn, vn, u, pool_new = _proj_sample_call(l, hs, mod, wts, tabs_s, pool_prev)
        yb = _attn_sample_call(l, q, cache_kt, cache_vt, kn, vn, sinks_s)
        yc, hre, him = _ssm_call(l, u, h0_re, h0_im, l, wts, DEC_BATCH, DEC_SEQ, False, "ssm_sample")
        hs, _ = _merge_call(l, True, hs, mod, ya, yb, yc, wts)
        hs = _ffn_call(l, True, last, hs, mod, wts, g_final)
        for k, v in zip(range(5, 10), (kn.reshape(DEC_BATCH, DEC_SEQ, KV_HEADS, HEAD_DIM),
                                       vn.reshape(DEC_BATCH, DEC_SEQ, KV_HEADS, HEAD_DIM),
                                       pool_new,
                                       hre.reshape(DEC_BATCH, SSM_GROUPS, SSM_STATE),
                                       him.reshape(DEC_BATCH, SSM_GROUPS, SSM_STATE))):
            outs[k].append(v)

    stacked = [jnp.stack(o) for o in outs]
    stacked[5] = jnp.concatenate([cache_win_k[:, :, DEC_SEQ:], stacked[5]], axis=2)
    stacked[6] = jnp.concatenate([cache_win_v[:, :, DEC_SEQ:], stacked[6]], axis=2)
    stacked[7] = jnp.transpose(stacked[7], (0, 2, 1, 3))
    return (hp, hs.reshape(DEC_BATCH, DEC_SEQ, D_MODEL), *stacked)
```
